```python
import jax
import jax.numpy as jnp
from jax import lax
import numpy as np

D_MODEL = 2048
BATCH = 1
SEQ = 16384
DEPTH = 4
DEC_BATCH = 4
DEC_SEQ = 2048
PAST_LEN = 128

D_MIX = D_MODEL
CONV_W = D_MIX // 2
CONV_GROUPS = 8
CONV_K = 3
MLSTM_W = D_MIX - CONV_W
MLSTM_HEADS = 4
MLSTM_DH = MLSTM_W // MLSTM_HEADS
CHUNK = 128
N_GATES = 4 * MLSTM_HEADS
N_IN = 3 * CONV_W + 4 * MLSTM_W + N_GATES
D_FF = (11 * D_MODEL) // 4
FFN_CONV_K = 3
EPS = 1e-6

kernel_name = 'hybrid_shortconv_mlstm_bidir_encoder'


def _rmsnorm(x, g):
    xf = x.astype(jnp.float32)
    y = xf * lax.rsqrt(jnp.mean(xf * xf, axis=-1, keepdims=True) + EPS)
    return (y * g.astype(jnp.float32)).astype(x.dtype)


def _group_rmsnorm(x, g, n_groups):
    shp = x.shape
    xf = x.astype(jnp.float32).reshape(shp[:-1] + (n_groups, shp[-1] // n_groups))
    xf = xf * lax.rsqrt(jnp.mean(xf * xf, axis=-1, keepdims=True) + EPS)
    return xf.reshape(shp) * g.astype(jnp.float32)


def _dwconv_centred(x, w):
    k_w = w.shape[0]
    s = x.shape[1]
    half = k_w // 2
    xp = jnp.pad(x, ((0, 0), (half, half), (0, 0)))
    w = w.astype(x.dtype)
    y = w[0] * xp[:, 0:s]
    for k in range(1, k_w):
        y = y + w[k] * xp[:, k:k + s]
    return y


def _mlstm_scan(q, k, v, i_pre, logf):
    b, h, s, dh = q.shape
    nc = s // CHUNK

    def to_chunks(t):
        t = t.reshape((b, h, nc, CHUNK) + t.shape[3:])
        return jnp.moveaxis(t, 2, 0)

    xs = (to_chunks(q), to_chunks(k), to_chunks(v), to_chunks(i_pre), to_chunks(logf))
    mask = jnp.tril(jnp.ones((CHUNK, CHUNK), dtype=bool))

    def step(carry, inp):
        c_s, n_s, m_s = carry
        qc, kc, vc, ic, lfc = inp
        bcum = jnp.cumsum(lfc, axis=-1)
        d_mat = bcum[..., :, None] - bcum[..., None, :] + ic[..., None, :]
        d_mat = jnp.where(mask, d_mat, -jnp.inf)
        inter = bcum + m_s[..., None]
        m_t = jnp.maximum(inter, jnp.max(d_mat, axis=-1))
        w_ts = jnp.exp(d_mat - m_t[..., None])
        scores = jnp.einsum('bhtd,bhsd->bhts', qc, kc) * w_ts
        a_t = jnp.exp(inter - m_t)
        num = a_t[..., None] * jnp.einsum('bhtd,bhde->bhte', qc, c_s) + jnp.einsum('bhts,bhse->bhte', scores, vc)
        den = a_t * jnp.einsum('bhtd,bhd->bht', qc, n_s) + jnp.sum(scores, axis=-1)
        h_out = num / jnp.maximum(jnp.abs(den), jnp.exp(-m_t))[..., None]
        b_last = bcum[..., -1]
        g_s = b_last[..., None] - bcum + ic
        m_new = jnp.maximum(b_last + m_s, jnp.max(g_s, axis=-1))
        a_c = jnp.exp(b_last + m_s - m_new)
        kw = kc * jnp.exp(g_s - m_new[..., None])[..., None]
        c_new = a_c[..., None, None] * c_s + jnp.einsum('bhsd,bhse->bhde', kw, vc)
        n_new = a_c[..., None] * n_s + jnp.sum(kw, axis=2)
        return (c_new, n_new, m_new), h_out

    init = (jnp.zeros((b, h, dh, dh), jnp.float32),
            jnp.zeros((b, h, dh), jnp.float32),
            jnp.zeros((b, h), jnp.float32))
    _, hs = lax.scan(step, init, xs)
    return jnp.moveaxis(hs, 0, 2).reshape(b, h, s, dh)


def _mixer(xn, w_in, gate_b, conv_w, conv_norm_g, mlstm_norm_g, w_out):
    b, s, _ = xn.shape
    proj = jnp.einsum('bsd,dn->bsn', xn, w_in.astype(xn.dtype))
    splits = np.cumsum([CONV_W] * 3 + [MLSTM_W] * 4).tolist()
    cb, cc, ch, q, k, v, o, gates = jnp.split(proj, splits, axis=-1)

    y_conv = cb * _dwconv_centred(cc * ch, conv_w)
    conv_out = _group_rmsnorm(y_conv, conv_norm_g, CONV_GROUPS)

    def heads(t):
        return t.reshape(b, s, MLSTM_HEADS, MLSTM_DH).transpose(0, 2, 1, 3).astype(jnp.float32)

    qh = heads(q) * (MLSTM_DH ** -0.5)
    kh = heads(k)
    vh = heads(v)
    g = (gates.astype(jnp.float32) + gate_b.astype(jnp.float32))
    g = g.reshape(b, s, 4, MLSTM_HEADS).transpose(2, 0, 3, 1)
    h_fwd = _mlstm_scan(qh, kh, vh, g[0], jax.nn.log_sigmoid(g[1]))

    def flip(t):
        return jnp.flip(t, axis=2)

    h_bwd = flip(_mlstm_scan(flip(qh), flip(kh), flip(vh), flip(g[2]), jax.nn.log_sigmoid(flip(g[3]))))
    hm = (h_fwd + h_bwd).transpose(0, 2, 1, 3).reshape(b, s, MLSTM_W)
    hm = _group_rmsnorm(hm, mlstm_norm_g, MLSTM_HEADS) * jax.nn.sigmoid(o.astype(jnp.float32))

    mixed = jnp.concatenate([conv_out, hm], axis=-1).astype(xn.dtype)
    return jnp.einsum('bsm,md->bsd', mixed, w_out.astype(xn.dtype))


def _conv_glu_ffn(xn, w_up, ffn_conv_w, ffn_conv_b, w_down):
    u = jnp.einsum('bsd,df->bsf', xn, w_up.astype(xn.dtype))
    u = _dwconv_centred(u, ffn_conv_w) + ffn_conv_b.astype(u.dtype)
    u_g, u_v = jnp.split(u, 2, axis=-1)
    return jnp.einsum('bsf,fd->bsd', jax.nn.silu(u_g) * u_v, w_down.astype(xn.dtype))


def _trunk(x, norm_mix_g, w_in, gate_b, conv_w, conv_norm_g, mlstm_norm_g, w_out,
           norm_ffn_g, w_up, ffn_conv_w, ffn_conv_b, w_down, final_norm_g):
    for l in range(DEPTH):
        x = x + _mixer(_rmsnorm(x, norm_mix_g[l]), w_in[l], gate_b[l], conv_w[l],
                       conv_norm_g[l], mlstm_norm_g[l], w_out[l])
        x = x + _conv_glu_ffn(_rmsnorm(x, norm_ffn_g[l]), w_up[l], ffn_conv_w[l],
                              ffn_conv_b[l], w_down[l])
    return _rmsnorm(x, final_norm_g)


def setup_inputs(seed: int = 0) -> dict:
    key = jax.random.key(seed)
    ks = jax.random.split(key, 16)
    f32 = jnp.float32

    def nrm(k, shape, scale):
        return jax.random.normal(k, shape, f32) * scale

    res_scale = (2 * DEPTH) ** -0.5
    x_prompt = nrm(ks[0], (BATCH, SEQ, D_MODEL), 1.0)
    x_sample = nrm(ks[1], (DEC_BATCH, DEC_SEQ, D_MODEL), 1.0)
    norm_mix_g = 1.0 + nrm(ks[2], (DEPTH, D_MODEL), 0.02)
    w_in = nrm(ks[3], (DEPTH, D_MODEL, N_IN), D_MODEL ** -0.5)
    f_bias = jnp.linspace(3.0, 6.0, MLSTM_HEADS, dtype=f32)
    zeros_h = jnp.zeros((MLSTM_HEADS,), f32)
    base_gb = jnp.stack([zeros_h, f_bias, zeros_h, f_bias])
    gate_b = (nrm(ks[4], (DEPTH, 4, MLSTM_HEADS), 0.1) + base_gb[None]).reshape(DEPTH, N_GATES)
    conv_w = nrm(ks[5], (DEPTH, CONV_K, CONV_W), CONV_K ** -0.5)
    conv_norm_g = 1.0 + nrm(ks[6], (DEPTH, CONV_W), 0.02)
    mlstm_norm_g = 1.0 + nrm(ks[7], (DEPTH, MLSTM_W), 0.02)
    w_out = nrm(ks[8], (DEPTH, D_MIX, D_MODEL), (D_MIX ** -0.5) * res_scale)
    norm_ffn_g = 1.0 + nrm(ks[9], (DEPTH, D_MODEL), 0.02)
    w_up = nrm(ks[10], (DEPTH, D_MODEL, 2 * D_FF), D_MODEL ** -0.5)
    ffn_conv_w = nrm(ks[11], (DEPTH, FFN_CONV_K, 2 * D_FF), FFN_CONV_K ** -0.5)
    ffn_conv_b = nrm(ks[12], (DEPTH, 2 * D_FF), 0.02)
    w_down = nrm(ks[13], (DEPTH, D_FF, D_MODEL), (D_FF ** -0.5) * res_scale)
    final_norm_g = 1.0 + nrm(ks[14], (D_MODEL,), 0.02)
    return {'x_prompt': x_prompt, 'x_sample': x_sample, 'norm_mix_g': norm_mix_g, 'w_in': w_in,
            'gate_b': gate_b, 'conv_w': conv_w, 'conv_norm_g': conv_norm_g,
            'mlstm_norm_g': mlstm_norm_g, 'w_out': w_out, 'norm_ffn_g': norm_ffn_g, 'w_up': w_up,
            'ffn_conv_w': ffn_conv_w, 'ffn_conv_b': ffn_conv_b, 'w_down': w_down,
            'final_norm_g': final_norm_g}


def reference(x_prompt, x_sample, norm_mix_g, w_in, gate_b, conv_w, conv_norm_g, mlstm_norm_g,
              w_out, norm_ffn_g, w_up, ffn_conv_w, ffn_conv_b, w_down, final_norm_g):
    y_prompt = _trunk(x_prompt, norm_mix_g, w_in, gate_b, conv_w, conv_norm_g, mlstm_norm_g,
                      w_out, norm_ffn_g, w_up, ffn_conv_w, ffn_conv_b, w_down, final_norm_g)
    y_sample = _trunk(x_sample, norm_mix_g, w_in, gate_b, conv_w, conv_norm_g, mlstm_norm_g,
                      w_out, norm_ffn_g, w_up, ffn_conv_w, ffn_conv_b, w_down, final_norm_g)
    return (y_prompt, y_sample)
```

```python
import functools

import numpy as np
import jax
import jax.numpy as jnp
from jax import lax
from jax.experimental import pallas as pl
from jax.experimental.pallas import tpu as pltpu

EPS = 1e-6
CHUNK = 128
CONV_GROUPS = 8
MLSTM_HEADS = 4
LANES = 128
HALO = 16
VMEM_LIMIT = 60 * 1024 * 1024

F32 = jnp.float32
BF16 = jnp.bfloat16


def _dot(a, b):
    return jnp.dot(a, b, preferred_element_type=F32)


def _rms(x, g):
    return x * lax.rsqrt(jnp.mean(x * x, axis=-1, keepdims=True) + EPS) * g


def _boundary_flags(seq_lens, tile):
    starts = np.cumsum([0] + list(seq_lens[:-1]))
    ends = np.cumsum(list(seq_lens))
    n = int(ends[-1]) // tile
    first = np.zeros((n,), np.int32)
    last = np.zeros((n,), np.int32)
    for s in starts:
        first[int(s) // tile] = 1
    for e in ends:
        last[int(e) // tile - 1] = 1
    return jnp.asarray(np.stack([first, last]))


def _in_proj_kernel(flags_ref, xp_ref, x_ref, xn_ref, g_ref,
                    wcb_ref, wcc_ref, wch_ref, wq_ref, wk_ref, wv_ref, wo_ref,
                    wgi_ref, wgf_ref, bgi_ref, bgf_ref, cw_ref, cg_ref,
                    conv_ref, q_ref, k_ref, v_ref, o_ref, gi_ref, gf_ref,
                    xs_ref, z_ref, *, tm, gw, q_scale):
    i = pl.program_id(0)
    j = pl.program_id(1)

    @pl.when(j == 0)
    def _():
        g = g_ref[...]
        keep_prev = jnp.where(flags_ref[0, i] == 1, 0.0, 1.0)
        keep_next = jnp.where(flags_ref[1, i] == 1, 0.0, 1.0)
        xs_ref[0:HALO, :] = (_rms(xp_ref[...], g) * keep_prev).astype(BF16)
        xs_ref[HALO:HALO + tm, :] = _rms(x_ref[...], g).astype(BF16)
        xs_ref[HALO + tm:, :] = (_rms(xn_ref[...], g) * keep_next).astype(BF16)
        xm0 = xs_ref[HALO:HALO + tm, :]
        gi_ref[...] = _dot(xm0, wgi_ref[...]) + bgi_ref[...]
        gf_ref[...] = _dot(xm0, wgf_ref[...]) + bgf_ref[...]

    xm = xs_ref[HALO:HALO + tm, :]
    xe = xs_ref[...]

    z_ref[...] = _dot(xe, wcc_ref[...]) * _dot(xe, wch_ref[...])
    cw = cw_ref[...]
    conv = (cw[0:1, :] * z_ref[pl.ds(HALO - 1, tm), :]
            + cw[1:2, :] * z_ref[pl.ds(HALO, tm), :]
            + cw[2:3, :] * z_ref[pl.ds(HALO + 1, tm), :])
    y = _dot(xm, wcb_ref[...]) * conv
    cg = cg_ref[...]
    for c0 in range(0, y.shape[1], gw):
        yg = y[:, c0:c0 + gw]
        inv = lax.rsqrt(jnp.mean(yg * yg, axis=-1, keepdims=True) + EPS)
        conv_ref[:, c0:c0 + gw] = (yg * inv * cg[:, c0:c0 + gw]).astype(BF16)

    q_ref[...] = (_dot(xm, wq_ref[...]) * q_scale).astype(BF16)
    k_ref[...] = _dot(xm, wk_ref[...])
    v_ref[...] = _dot(xm, wv_ref[...]).astype(BF16)
    o_ref[...] = _dot(xm, wo_ref[...])


def _in_proj(x, flags, layer, norm_g, w_in, w_gi, w_gf, b_gi, b_gf, conv_w, conv_g, *, tm, bw):
    t, d = x.shape
    cw_total = conv_w.shape[-1]
    nj = cw_total // bw
    nt = t // tm
    hb = tm // HALO
    n_halo = t // HALO
    gw = cw_total // CONV_GROUPS
    dh = cw_total // MLSTM_HEADS

    def wspec(k):
        return pl.BlockSpec((None, d, bw), lambda i, j, f, k=k: (layer, 0, k * nj + j))

    col = lambda i, j, f: (i, j)
    const3 = lambda i, j, f: (layer, 0, 0)
    in_specs = [
        pl.BlockSpec((HALO, d), lambda i, j, f: (jnp.maximum(i * hb - 1, 0), 0)),
        pl.BlockSpec((tm, d), lambda i, j, f: (i, 0)),
        pl.BlockSpec((HALO, d), lambda i, j, f: (jnp.minimum((i + 1) * hb, n_halo - 1), 0)),
        pl.BlockSpec((None, 1, d), const3),
        wspec(0), wspec(1), wspec(2), wspec(3), wspec(4), wspec(5), wspec(6),
        pl.BlockSpec((None, d, LANES), const3),
        pl.BlockSpec((None, d, LANES), const3),
        pl.BlockSpec((None, 1, LANES), const3),
        pl.BlockSpec((None, 1, LANES), const3),
        pl.BlockSpec((None, 3, bw), lambda i, j, f: (layer, 0, j)),
        pl.BlockSpec((None, 1, bw), lambda i, j, f: (layer, 0, j)),
    ]
    out_specs = [
        pl.BlockSpec((tm, bw), col),
        pl.BlockSpec((tm, bw), col),
        pl.BlockSpec((tm, bw), col),
        pl.BlockSpec((tm, bw), col),
        pl.BlockSpec((tm, bw), col),
        pl.BlockSpec((tm, LANES), lambda i, j, f: (i, 0)),
        pl.BlockSpec((tm, LANES), lambda i, j, f: (i, 0)),
    ]
    out_shape = [
        jax.ShapeDtypeStruct((t, cw_total), BF16),
        jax.ShapeDtypeStruct((t, cw_total), BF16),
        jax.ShapeDtypeStruct((t, cw_total), F32),
        jax.ShapeDtypeStruct((t, cw_total), BF16),
        jax.ShapeDtypeStruct((t, cw_total), F32),
        jax.ShapeDtypeStruct((t, LANES), F32),
        jax.ShapeDtypeStruct((t, LANES), F32),
    ]
    kern = functools.partial(_in_proj_kernel, tm=tm, gw=gw, q_scale=float(dh) ** -0.5)
    return pl.pallas_call(
        kern,
        grid_spec=pltpu.PrefetchScalarGridSpec(
            num_scalar_prefetch=1, grid=(nt, nj), in_specs=in_specs, out_specs=out_specs,
            scratch_shapes=[pltpu.VMEM((tm + 2 * HALO, d), BF16),
                            pltpu.VMEM((tm + 2 * HALO, bw), F32)]),
        out_shape=out_shape,
        compiler_params=pltpu.CompilerParams(
            dimension_semantics=("arbitrary", "arbitrary"), vmem_limit_bytes=VMEM_LIMIT),
        name="in_proj",
    )(flags, x, x, x, norm_g, w_in, w_in, w_in, w_in, w_in, w_in, w_in,
      w_gi, w_gf, b_gi, b_gf, conv_w, conv_g)


def _split3(x):
    hi = x.astype(BF16)
    r1 = x - hi.astype(F32)
    mid = r1.astype(BF16)
    lo = (r1 - mid.astype(F32)).astype(BF16)
    return hi, mid, lo


def _gate_prep_kernel(gi_ref, gf_ref, bc_ref, gc_ref, sc_ref, rr_ref, *, rb, heads):
    L = CHUNK
    row = lax.broadcasted_iota(jnp.int32, (L, L), 0)
    col = lax.broadcasted_iota(jnp.int32, (L, L), 1)
    tri_pre = (col <= row).astype(BF16)
    tri_suf = (col >= row).astype(BF16)
    fwd_lane = lax.broadcasted_iota(jnp.int32, (1, LANES), 1) < heads
    for c in range(rb // L):
        rows = pl.ds(c * L, L)
        gi = gi_ref[rows, :]
        gf = gf_ref[rows, :]
        logf = jnp.minimum(gf, 0.0) - jnp.log1p(jnp.exp(-jnp.abs(gf)))
        hi, mid, lo = _split3(logf)
        pre = _dot(tri_pre, hi) + _dot(tri_pre, mid) + _dot(tri_pre, lo)
        suf = _dot(tri_suf, hi) + _dot(tri_suf, mid) + _dot(tri_suf, lo)
        bc = jnp.where(fwd_lane, pre, suf)
        bl = jnp.where(fwd_lane, pre[L - 1:L, :], suf[0:1, :])
        gc = bl - bc + gi
        gm = jnp.max(gc, axis=0, keepdims=True)
        bc_ref[rows, :] = bc
        gc_ref[rows, :] = gc
        sc_ref[c] = jnp.concatenate([bl, gm, jnp.zeros((6, LANES), F32)], axis=0)
        rr_ref[c] = (gi - bc).T[0:8, :]


def _gate_prep(gi, gf, *, rb, heads):
    t = gi.shape[0]
    nb = t // rb
    ncb = rb // CHUNK
    row_spec = pl.BlockSpec((rb, LANES), lambda i: (i, 0))
    ch_spec = pl.BlockSpec((ncb, 8, LANES), lambda i: (i, 0, 0))
    return pl.pallas_call(
        functools.partial(_gate_prep_kernel, rb=rb, heads=heads),
        grid=(nb,),
        in_specs=[row_spec, row_spec],
        out_specs=[row_spec, row_spec, ch_spec, ch_spec],
        out_shape=[jax.ShapeDtypeStruct((t, LANES), F32),
                   jax.ShapeDtypeStruct((t, LANES), F32),
                   jax.ShapeDtypeStruct((t // CHUNK, 8, LANES), F32),
                   jax.ShapeDtypeStruct((t // CHUNK, 8, LANES), F32)],
        compiler_params=pltpu.CompilerParams(dimension_semantics=("arbitrary",)),
        name="gate_prep",
    )(gi, gf)


def _lane_pick(tile, sel):
    return jnp.sum(jnp.where(sel, tile, 0.0), axis=1, keepdims=True)


def _mlstm_chunk(q, k, v, bc, gc, r_row, bl, gm, tri, c_ref, n_ref, m_ref, l):
    m = m_ref[l][0:1, 0:1]
    c_state = c_ref[l]
    n_state = n_ref[l]
    dmat = jnp.where(tri, bc + r_row, -jnp.inf)
    inter = bc + m
    m_t = jnp.maximum(inter, jnp.max(dmat, axis=1, keepdims=True))
    w = jnp.exp(dmat - m_t)
    s = lax.dot_general(q, k.astype(BF16), (((1,), (1,)), ((), ())), preferred_element_type=F32)
    scores = s * w
    a_t = jnp.exp(inter - m_t)
    num = a_t * _dot(q, c_state.astype(BF16)) + _dot(scores.astype(BF16), v)
    qn = jnp.sum(q.astype(F32) * n_state, axis=1, keepdims=True)
    den = a_t * qn + jnp.sum(scores, axis=1, keepdims=True)
    h = num / jnp.maximum(jnp.abs(den), jnp.exp(-m_t))
    m_new = jnp.maximum(bl + m, gm)
    a_c = jnp.exp(bl + m - m_new)
    kw = k * jnp.exp(gc - m_new)
    c_ref[l] = a_c * c_state + lax.dot_general(
        kw.astype(BF16), v, (((0,), (0,)), ((), ())), preferred_element_type=F32)
    n_ref[l] = a_c * n_state + jnp.sum(kw, axis=0, keepdims=True)
    m_ref[l] = jnp.broadcast_to(m_new, m_ref.shape[1:])
    return h


def _mlstm_kernel(flags_ref,
                  qf_ref, kf_ref, vf_ref, bcf_ref, gcf_ref, scf_ref, rrf_ref,
                  qb_ref, kb_ref, vb_ref, bcb_ref, gcb_ref, scb_ref, rrb_ref,
                  hf_ref, hb_ref, c_ref, n_ref, m_ref, *, rb, heads, dh):
    L = CHUNK
    nc = rb // L
    i = pl.program_id(0)
    nb = pl.num_programs(0)

    def reset(lo):
        for l in range(lo, lo + heads):
            c_ref[l] = jnp.zeros(c_ref.shape[1:], F32)
            n_ref[l] = jnp.zeros(n_ref.shape[1:], F32)
            m_ref[l] = jnp.zeros(m_ref.shape[1:], F32)

    @pl.when(flags_ref[0, i] == 1)
    def _():
        reset(0)

    @pl.when(flags_ref[1, nb - 1 - i] == 1)
    def _():
        reset(heads)

    row = lax.broadcasted_iota(jnp.int32, (L, L), 0)
    col = lax.broadcasted_iota(jnp.int32, (L, L), 1)
    lane = lax.broadcasted_iota(jnp.int32, (1, LANES), 1)
    dirs = (
        (col <= row, qf_ref, kf_ref, vf_ref, bcf_ref, gcf_ref, scf_ref, rrf_ref, hf_ref),
        (col >= row, qb_ref, kb_ref, vb_ref, bcb_ref, gcb_ref, scb_ref, rrb_ref, hb_ref),
    )

    def body(c, carry):
        for d, (tri, q_ref, k_ref, v_ref, bc_ref, gc_ref, sc_ref, rr_ref, h_ref) in enumerate(dirs):
            cc = c if d == 0 else nc - 1 - c
            rows = pl.ds(pl.multiple_of(cc * L, L), L)
            bct = bc_ref[rows, :]
            gct = gc_ref[rows, :]
            sc = sc_ref[cc]
            rr = rr_ref[cc]
            for hd in range(heads):
                l = d * heads + hd
                sel = lane == l
                cols = slice(hd * dh, (hd + 1) * dh)
                h = _mlstm_chunk(
                    q_ref[rows, cols], k_ref[rows, cols], v_ref[rows, cols],
                    _lane_pick(bct, sel), _lane_pick(gct, sel), rr[l:l + 1, :],
                    _lane_pick(sc[0:1, :], sel), _lane_pick(sc[1:2, :], sel),
                    tri, c_ref, n_ref, m_ref, l)
                h_ref[rows, cols] = h
        return carry

    lax.fori_loop(0, nc, body, 0)


def _mlstm(flags, q, k, v, bc, gc, sc, rr, *, rb, heads):
    t, mw = q.shape
    dh = mw // heads
    nb = t // rb
    ncb = rb // CHUNK
    fwd = lambda i, f: (i, 0)
    bwd = lambda i, f: (nb - 1 - i, 0)
    fwd3 = lambda i, f: (i, 0, 0)
    bwd3 = lambda i, f: (nb - 1 - i, 0, 0)

    def specs(m2, m3):
        return [pl.BlockSpec((rb, mw), m2), pl.BlockSpec((rb, mw), m2), pl.BlockSpec((rb, mw), m2),
                pl.BlockSpec((rb, LANES), m2), pl.BlockSpec((rb, LANES), m2),
                pl.BlockSpec((ncb, 8, LANES), m3), pl.BlockSpec((ncb, 8, LANES), m3)]

    kern = functools.partial(_mlstm_kernel, rb=rb, heads=heads, dh=dh)
    return pl.pallas_call(
        kern,
        grid_spec=pltpu.PrefetchScalarGridSpec(
            num_scalar_prefetch=1, grid=(nb,),
            in_specs=specs(fwd, fwd3) + specs(bwd, bwd3),
            out_specs=[pl.BlockSpec((rb, mw), fwd), pl.BlockSpec((rb, mw), bwd)],
            scratch_shapes=[pltpu.VMEM((2 * heads, dh, dh), F32),
                            pltpu.VMEM((2 * heads, 1, dh), F32),
                            pltpu.VMEM((2 * heads, 8, LANES), F32)]),
        out_shape=[jax.ShapeDtypeStruct((t, mw), F32), jax.ShapeDtypeStruct((t, mw), F32)],
        compiler_params=pltpu.CompilerParams(
            dimension_semantics=("arbitrary",), vmem_limit_bytes=VMEM_LIMIT),
        name="mlstm",
    )(flags, q, k, v, bc, gc, sc, rr, q, k, v, bc, gc, sc, rr)


def _combine_kernel(hf_ref, hb_ref, o_ref, g_ref, out_ref, *, heads, dh):
    g = g_ref[...]
    for hd in range(heads):
        cols = slice(hd * dh, (hd + 1) * dh)
        hs = hf_ref[:, cols] + hb_ref[:, cols]
        inv = lax.rsqrt(jnp.mean(hs * hs, axis=-1, keepdims=True) + EPS)
        out_ref[:, cols] = (hs * inv * g[:, cols] * jax.nn.sigmoid(o_ref[:, cols])).astype(BF16)


def _combine(hf, hb, o, layer, norm_g, *, tr, heads):
    t, mw = hf.shape
    spec = pl.BlockSpec((tr, mw), lambda i: (i, 0))
    return pl.pallas_call(
        functools.partial(_combine_kernel, heads=heads, dh=mw // heads),
        grid=(t // tr,),
        in_specs=[spec, spec, spec, pl.BlockSpec((None, 1, mw), lambda i: (layer, 0, 0))],
        out_specs=spec,
        out_shape=jax.ShapeDtypeStruct((t, mw), BF16),
        compiler_params=pltpu.CompilerParams(dimension_semantics=("arbitrary",)),
        name="combine",
    )(hf, hb, o, norm_g)


def _out_ffn_kernel(flags_ref,
                    xp_ref, x_ref, xn_ref, cp_ref, c_ref, cn_ref, hp_ref, h_ref, hn_ref,
                    woa_ref, wob_ref, g_ref, wug_ref, wuv_ref, cwg_ref, cwv_ref,
                    cbg_ref, cbv_ref, wd_ref, fg_ref,
                    out_ref, xs_ref, ug_ref, uv_ref, *, tm, final_norm):
    i = pl.program_id(0)
    j = pl.program_id(1)

    @pl.when(j == 0)
    def _():
        ce = jnp.concatenate([cp_ref[...], c_ref[...], cn_ref[...]], axis=0)
        he = jnp.concatenate([hp_ref[...], h_ref[...], hn_ref[...]], axis=0)
        proj = _dot(ce, woa_ref[...]) + _dot(he, wob_ref[...])
        g = g_ref[...]
        keep_prev = jnp.where(flags_ref[0, i] == 1, 0.0, 1.0)
        keep_next = jnp.where(flags_ref[1, i] == 1, 0.0, 1.0)
        x_mid = x_ref[...] + proj[HALO:HALO + tm, :]
        out_ref[...] = x_mid
        xs_ref[HALO:HALO + tm, :] = _rms(x_mid, g).astype(BF16)
        xs_ref[0:HALO, :] = (_rms(xp_ref[...] + proj[0:HALO, :], g) * keep_prev).astype(BF16)
        xs_ref[HALO + tm:, :] = (_rms(xn_ref[...] + proj[HALO + tm:, :], g) * keep_next).astype(BF16)

    xe = xs_ref[...]
    ug_ref[...] = _dot(xe, wug_ref[...])
    uv_ref[...] = _dot(xe, wuv_ref[...])

    def conv(u_ref, cw_ref, cb_ref):
        cw = cw_ref[...]
        return (cw[0:1, :] * u_ref[pl.ds(HALO - 1, tm), :]
                + cw[1:2, :] * u_ref[pl.ds(HALO, tm), :]
                + cw[2:3, :] * u_ref[pl.ds(HALO + 1, tm), :]) + cb_ref[...]

    u_g = conv(ug_ref, cwg_ref, cbg_ref)
    u_v = conv(uv_ref, cwv_ref, cbv_ref)
    hidden = (u_g * jax.nn.sigmoid(u_g) * u_v).astype(BF16)
    out_ref[...] += _dot(hidden, wd_ref[...])

    if final_norm:
        @pl.when(j == pl.num_programs(1) - 1)
        def _():
            out_ref[...] = _rms(out_ref[...], fg_ref[...])


def _out_ffn(x, conv_out, hm, flags, layer, w_out, norm_g, w_up, ffn_cw, ffn_cb, w_down, final_g,
             *, tm, bf, final_norm):
    t, d = x.shape
    cw = conv_out.shape[1]
    mw = hm.shape[1]
    d_ff = w_down.shape[1]
    nf = d_ff // bf
    nt = t // tm
    hb = tm // HALO
    n_halo = t // HALO
    prev = lambda i, j, f: (jnp.maximum(i * hb - 1, 0), 0)
    main = lambda i, j, f: (i, 0)
    nxt = lambda i, j, f: (jnp.minimum((i + 1) * hb, n_halo - 1), 0)
    once = pl.Buffered(1)

    def triple(width):
        return [pl.BlockSpec((HALO, width), prev), pl.BlockSpec((tm, width), main),
                pl.BlockSpec((HALO, width), nxt)]

    in_specs = triple(d) + triple(cw) + triple(mw) + [
        pl.BlockSpec((None, cw, d), lambda i, j, f: (layer, 0, 0), pipeline_mode=once),
        pl.BlockSpec((None, mw, d), lambda i, j, f: (layer, cw // mw, 0), pipeline_mode=once),
        pl.BlockSpec((None, 1, d), lambda i, j, f: (layer, 0, 0)),
        pl.BlockSpec((None, d, bf), lambda i, j, f: (layer, 0, j)),
        pl.BlockSpec((None, d, bf), lambda i, j, f: (layer, 0, nf + j)),
        pl.BlockSpec((None, 3, bf), lambda i, j, f: (layer, 0, j)),
        pl.BlockSpec((None, 3, bf), lambda i, j, f: (layer, 0, nf + j)),
        pl.BlockSpec((None, 1, bf), lambda i, j, f: (layer, 0, j)),
        pl.BlockSpec((None, 1, bf), lambda i, j, f: (layer, 0, nf + j)),
        pl.BlockSpec((None, bf, d), lambda i, j, f: (layer, j, 0)),
        pl.BlockSpec((1, d), lambda i, j, f: (0, 0)),
    ]
    kern = functools.partial(_out_ffn_kernel, tm=tm, final_norm=final_norm)
    return pl.pallas_call(
        kern,
        grid_spec=pltpu.PrefetchScalarGridSpec(
            num_scalar_prefetch=1, grid=(nt, nf), in_specs=in_specs,
            out_specs=pl.BlockSpec((tm, d), main),
            scratch_shapes=[pltpu.VMEM((tm + 2 * HALO, d), BF16),
                            pltpu.VMEM((tm + 2 * HALO, bf), F32),
                            pltpu.VMEM((tm + 2 * HALO, bf), F32)]),
        out_shape=jax.ShapeDtypeStruct((t, d), F32),
        compiler_params=pltpu.CompilerParams(
            dimension_semantics=("arbitrary", "arbitrary"), vmem_limit_bytes=VMEM_LIMIT),
        name="out_ffn",
    )(flags, x, x, x, conv_out, conv_out, conv_out, hm, hm, hm,
      w_out, w_out, norm_g, w_up, w_up, ffn_cw, ffn_cw, ffn_cb, ffn_cb, w_down, final_g)


def _tile_sizes(seq_lens):
    g = int(np.gcd.reduce(np.asarray(seq_lens)))
    tm = min(512, g)
    rb = min(512, g)
    return tm, rb


def _trunk(x, seq_lens, norm_mix_g, w_in, gate_b, conv_w, conv_norm_g, mlstm_norm_g, w_out,
           norm_ffn_g, w_up, ffn_conv_w, ffn_conv_b, w_down, final_norm_g, tm=None, rb=None):
    depth, d, n_in = w_in.shape
    cw = conv_w.shape[-1]
    mw = mlstm_norm_g.shape[-1]
    heads = MLSTM_HEADS
    assert cw == mw and n_in == 3 * cw + 4 * mw + 4 * heads
    d_ff = w_down.shape[1]
    auto_tm, auto_rb = _tile_sizes(seq_lens)
    tm = tm or auto_tm
    rb = rb or auto_rb
    bw = 256 if cw % 256 == 0 else LANES
    bf = 512 if d_ff % 512 == 0 else LANES

    flags_tm = _boundary_flags(seq_lens, tm)
    flags_rb = _boundary_flags(seq_lens, rb)

    w_in_b = w_in.astype(BF16)
    w_out_b = w_out.astype(BF16)
    w_up_b = w_up.astype(BF16)
    w_down_b = w_down.astype(BF16)
    g0 = 3 * cw + 4 * mw
    i_cols = np.concatenate([np.arange(0, heads), np.arange(2 * heads, 3 * heads)]) + g0
    f_cols = i_cols + heads
    pad = ((0, 0), (0, 0), (0, LANES - 2 * heads))
    w_gi = jnp.pad(w_in[:, :, i_cols], pad).astype(BF16)
    w_gf = jnp.pad(w_in[:, :, f_cols], pad).astype(BF16)
    b_gi = jnp.pad(gate_b[:, i_cols - g0], pad[1:])[:, None, :]
    b_gf = jnp.pad(gate_b[:, f_cols - g0], pad[1:])[:, None, :]

    norm_mix_g = norm_mix_g[:, None, :]
    conv_norm_g = conv_norm_g[:, None, :]
    mlstm_norm_g = mlstm_norm_g[:, None, :]
    norm_ffn_g = norm_ffn_g[:, None, :]
    ffn_conv_b = ffn_conv_b[:, None, :]
    final_g = final_norm_g[None, :]

    for layer in range(depth):
        conv_out, q, k, v, o, gi, gf = _in_proj(
            x, flags_tm, layer, norm_mix_g, w_in_b, w_gi, w_gf, b_gi, b_gf, conv_w, conv_norm_g,
            tm=tm, bw=bw)
        bc, gc, sc, rr = _gate_prep(gi, gf, rb=rb, heads=heads)
        hf, hb = _mlstm(flags_rb, q, k, v, bc, gc, sc, rr, rb=rb, heads=heads)
        hm = _combine(hf, hb, o, layer, mlstm_norm_g, tr=tm, heads=heads)
        x = _out_ffn(x, conv_out, hm, flags_tm, layer, w_out_b, norm_ffn_g, w_up_b, ffn_conv_w,
                     ffn_conv_b, w_down_b, final_g, tm=tm, bf=bf, final_norm=(layer == depth - 1))
    return x


def kernel(x_prompt, x_sample, norm_mix_g, w_in, gate_b, conv_w, conv_norm_g, mlstm_norm_g, w_out,
           norm_ffn_g, w_up, ffn_conv_w, ffn_conv_b, w_down, final_norm_g):
    bp, sp, d = x_prompt.shape
    bs, ss, _ = x_sample.shape
    seq_lens = [sp] * bp + [ss] * bs
    x = jnp.concatenate([x_prompt.reshape(bp * sp, d), x_sample.reshape(bs * ss, d)], axis=0)
    y = _trunk(x, seq_lens, norm_mix_g, w_in, gate_b, conv_w, conv_norm_g, mlstm_norm_g, w_out,
               norm_ffn_g, w_up, ffn_conv_w, ffn_conv_b, w_down, final_norm_g)
    return (y[:bp * sp].reshape(bp, sp, d), y[bp * sp:].reshape(bs, ss, d))
```

```python
import functools

import numpy as np
import jax
import jax.numpy as jnp
from jax import lax
from jax.experimental import pallas as pl
from jax.experimental.pallas import tpu as pltpu

EPS = 1e-6
CHUNK = 128
CONV_GROUPS = 8
MLSTM_HEADS = 4
LANES = 128
HALO = 16
VMEM_LIMIT = 60 * 1024 * 1024

F32 = jnp.float32
BF16 = jnp.bfloat16


def _dot(a, b):
    return jnp.dot(a, b, preferred_element_type=F32)


def _rms(x, g):
    return x * lax.rsqrt(jnp.mean(x * x, axis=-1, keepdims=True) + EPS) * g


def _boundary_flags(seq_lens, tile):
    starts = np.cumsum([0] + list(seq_lens[:-1]))
    ends = np.cumsum(list(seq_lens))
    n = int(ends[-1]) // tile
    first = np.zeros((n,), np.int32)
    last = np.zeros((n,), np.int32)
    for s in starts:
        first[int(s) // tile] = 1
    for e in ends:
        last[int(e) // tile - 1] = 1
    return jnp.asarray(np.stack([first, last]))


def _in_proj_kernel(flags_ref, xp_ref, x_ref, xn_ref, g_ref,
                    wcb_ref, wcc_ref, wch_ref, wq_ref, wk_ref, wv_ref, wo_ref,
                    wgi_ref, wgf_ref, bgi_ref, bgf_ref, cw_ref, cg_ref,
                    conv_ref, q_ref, k_ref, v_ref, o_ref, gi_ref, gf_ref,
                    xs_ref, z_ref, *, tm, gw, q_scale):
    i = pl.program_id(0)
    j = pl.program_id(1)

    @pl.when(j == 0)
    def _():
        g = g_ref[...]
        keep_prev = jnp.where(flags_ref[0, i] == 1, 0.0, 1.0)
        keep_next = jnp.where(flags_ref[1, i] == 1, 0.0, 1.0)
        xs_ref[0:HALO, :] = (_rms(xp_ref[...], g) * keep_prev).astype(BF16)
        xs_ref[HALO:HALO + tm, :] = _rms(x_ref[...], g).astype(BF16)
        xs_ref[HALO + tm:, :] = (_rms(xn_ref[...], g) * keep_next).astype(BF16)
        xm0 = xs_ref[HALO:HALO + tm, :]
        gi_ref[...] = _dot(xm0, wgi_ref[...]) + bgi_ref[...]
        gf_ref[...] = _dot(xm0, wgf_ref[...]) + bgf_ref[...]

    xm = xs_ref[HALO:HALO + tm, :]
    xe = xs_ref[...]

    z_ref[...] = _dot(xe, wcc_ref[...]) * _dot(xe, wch_ref[...])
    cw = cw_ref[...]
    conv = (cw[0:1, :] * z_ref[pl.ds(HALO - 1, tm), :]
            + cw[1:2, :] * z_ref[pl.ds(HALO, tm), :]
            + cw[2:3, :] * z_ref[pl.ds(HALO + 1, tm), :])
    y = _dot(xm, wcb_ref[...]) * conv
    cg = cg_ref[...]
    for c0 in range(0, y.shape[1], gw):
        yg = y[:, c0:c0 + gw]
        inv = lax.rsqrt(jnp.mean(yg * yg, axis=-1, keepdims=True) + EPS)
        conv_ref[:, c0:c0 + gw] = (yg * inv * cg[:, c0:c0 + gw]).astype(BF16)

    q_ref[...] = (_dot(xm, wq_ref[...]) * q_scale).astype(BF16)
    k_ref[...] = _dot(xm, wk_ref[...])
    v_ref[...] = _dot(xm, wv_ref[...]).astype(BF16)
    o_ref[...] = _dot(xm, wo_ref[...])


def _in_proj(x, flags, layer, norm_g, w_in, w_gi, w_gf, b_gi, b_gf, conv_w, conv_g, *, tm, bw):
    t, d = x.shape
    cw_total = conv_w.shape[-1]
    nj = cw_total // bw
    nt = t // tm
    hb = tm // HALO
    n_halo = t // HALO
    gw = cw_total // CONV_GROUPS
    dh = cw_total // MLSTM_HEADS

    def wspec(k):
        return pl.BlockSpec((None, d, bw), lambda i, j, f, k=k: (layer, 0, k * nj + j))

    col = lambda i, j, f: (i, j)
    const3 = lambda i, j, f: (layer, 0, 0)
    in_specs = [
        pl.BlockSpec((HALO, d), lambda i, j, f: (jnp.maximum(i * hb - 1, 0), 0)),
        pl.BlockSpec((tm, d), lambda i, j, f: (i, 0)),
        pl.BlockSpec((HALO, d), lambda i, j, f: (jnp.minimum((i + 1) * hb, n_halo - 1), 0)),
        pl.BlockSpec((None, 1, d), const3),
        wspec(0), wspec(1), wspec(2), wspec(3), wspec(4), wspec(5), wspec(6),
        pl.BlockSpec((None, d, LANES), const3),
        pl.BlockSpec((None, d, LANES), const3),
        pl.BlockSpec((None, 1, LANES), const3),
        pl.BlockSpec((None, 1, LANES), const3),
        pl.BlockSpec((None, 3, bw), lambda i, j, f: (layer, 0, j)),
        pl.BlockSpec((None, 1, bw), lambda i, j, f: (layer, 0, j)),
    ]
    out_specs = [
        pl.BlockSpec((tm, bw), col),
        pl.BlockSpec((tm, bw), col),
        pl.BlockSpec((tm, bw), col),
        pl.BlockSpec((tm, bw), col),
        pl.BlockSpec((tm, bw), col),
        pl.BlockSpec((tm, LANES), lambda i, j, f: (i, 0)),
        pl.BlockSpec((tm, LANES), lambda i, j, f: (i, 0)),
    ]
    out_shape = [
        jax.ShapeDtypeStruct((t, cw_total), BF16),
        jax.ShapeDtypeStruct((t, cw_total), BF16),
        jax.ShapeDtypeStruct((t, cw_total), F32),
        jax.ShapeDtypeStruct((t, cw_total), BF16),
        jax.ShapeDtypeStruct((t, cw_total), F32),
        jax.ShapeDtypeStruct((t, LANES), F32),
        jax.ShapeDtypeStruct((t, LANES), F32),
    ]
    kern = functools.partial(_in_proj_kernel, tm=tm, gw=gw, q_scale=float(dh) ** -0.5)
    return pl.pallas_call(
        kern,
        grid_spec=pltpu.PrefetchScalarGridSpec(
            num_scalar_prefetch=1, grid=(nt, nj), in_specs=in_specs, out_specs=out_specs,
            scratch_shapes=[pltpu.VMEM((tm + 2 * HALO, d), BF16),
                            pltpu.VMEM((tm + 2 * HALO, bw), F32)]),
        out_shape=out_shape,
        compiler_params=pltpu.CompilerParams(
            dimension_semantics=("arbitrary", "arbitrary"), vmem_limit_bytes=VMEM_LIMIT),
        name="in_proj",
    )(flags, x, x, x, norm_g, w_in, w_in, w_in, w_in, w_in, w_in, w_in,
      w_gi, w_gf, b_gi, b_gf, conv_w, conv_g)


def _split3(x):
    hi = x.astype(BF16)
    r1 = x - hi.astype(F32)
    mid = r1.astype(BF16)
    lo = (r1 - mid.astype(F32)).astype(BF16)
    return hi, mid, lo


def _gate_prep_kernel(gi_ref, gf_ref, bc_ref, gc_ref, dm_ref, sc_ref, rr_ref, *, rb, heads):
    L = CHUNK
    row = lax.broadcasted_iota(jnp.int32, (L, L), 0)
    col = lax.broadcasted_iota(jnp.int32, (L, L), 1)
    causal = (col <= row, col >= row)
    tri_pre = causal[0].astype(BF16)
    tri_suf = causal[1].astype(BF16)
    lane = lax.broadcasted_iota(jnp.int32, (1, LANES), 1)
    fwd_lane = lane < heads
    for c in range(rb // L):
        rows = pl.ds(c * L, L)
        gi = gi_ref[rows, :]
        gf = gf_ref[rows, :]
        logf = jnp.minimum(gf, 0.0) - jnp.log1p(jnp.exp(-jnp.abs(gf)))
        hi, mid, lo = _split3(logf)
        pre = _dot(tri_pre, hi) + _dot(tri_pre, mid) + _dot(tri_pre, lo)
        suf = _dot(tri_suf, hi) + _dot(tri_suf, mid) + _dot(tri_suf, lo)
        bc = jnp.where(fwd_lane, pre, suf)
        bl = jnp.where(fwd_lane, pre[L - 1:L, :], suf[0:1, :])
        gc = bl - bc + gi
        gm = jnp.max(gc, axis=0, keepdims=True)
        r_rows = (gi - bc).T[0:8, :]
        cm = jnp.zeros((L, LANES), F32)
        for l in range(2 * heads):
            masked = jnp.where(causal[l // heads], r_rows[l:l + 1, :], -jnp.inf)
            cm = jnp.where(lane == l, jnp.max(masked, axis=1, keepdims=True), cm)
        bc_ref[rows, :] = bc
        gc_ref[rows, :] = gc
        dm_ref[rows, :] = bc + cm
        sc_ref[c] = jnp.concatenate([bl, gm, jnp.zeros((6, LANES), F32)], axis=0)
        rr_ref[c] = r_rows


def _gate_prep(gi, gf, *, rb, heads):
    t = gi.shape[0]
    nb = t // rb
    ncb = rb // CHUNK
    row_spec = pl.BlockSpec((rb, LANES), lambda i: (i, 0))
    ch_spec = pl.BlockSpec((ncb, 8, LANES), lambda i: (i, 0, 0))
    return pl.pallas_call(
        functools.partial(_gate_prep_kernel, rb=rb, heads=heads),
        grid=(nb,),
        in_specs=[row_spec, row_spec],
        out_specs=[row_spec, row_spec, row_spec, ch_spec, ch_spec],
        out_shape=[jax.ShapeDtypeStruct((t, LANES), F32),
                   jax.ShapeDtypeStruct((t, LANES), F32),
                   jax.ShapeDtypeStruct((t, LANES), F32),
                   jax.ShapeDtypeStruct((t // CHUNK, 8, LANES), F32),
                   jax.ShapeDtypeStruct((t // CHUNK, 8, LANES), F32)],
        compiler_params=pltpu.CompilerParams(dimension_semantics=("arbitrary",)),
        name="gate_prep",
    )(gi, gf)


def _mscan_kernel(flags_ref, sc_ref, ms_ref, *, nch, heads):
    fwd_lane = lax.broadcasted_iota(jnp.int32, (1, LANES), 1) < heads
    pad = jnp.zeros((5, LANES), F32)

    def step(c, m, reset):
        m = jnp.where(reset, 0.0, m)
        sc = sc_ref[c]
        bl = sc[0:1, :]
        m_new = jnp.maximum(bl + m, sc[1:2, :])
        return m, m_new, jnp.exp(bl + m - m_new)

    def fwd(c, m):
        m, m_new, a_c = step(c, m, flags_ref[0, c] == 1)
        ms_ref[c] = jnp.concatenate([m, m_new, a_c, pad], axis=0)
        return m_new

    def bwd(k, m):
        c = nch - 1 - k
        m, m_new, a_c = step(c, m, flags_ref[1, c] == 1)
        ms_ref[c] = jnp.where(fwd_lane, ms_ref[c], jnp.concatenate([m, m_new, a_c, pad], axis=0))
        return m_new

    zero = jnp.zeros((1, LANES), F32)
    lax.fori_loop(0, nch, fwd, zero)
    lax.fori_loop(0, nch, bwd, zero)


def _mscan(flags, sc, *, heads):
    nch = sc.shape[0]
    full = pl.BlockSpec(sc.shape, lambda i, f: (0, 0, 0))
    return pl.pallas_call(
        functools.partial(_mscan_kernel, nch=nch, heads=heads),
        grid_spec=pltpu.PrefetchScalarGridSpec(
            num_scalar_prefetch=1, grid=(1,), in_specs=[full], out_specs=full),
        out_shape=jax.ShapeDtypeStruct(sc.shape, F32),
        compiler_params=pltpu.CompilerParams(dimension_semantics=("arbitrary",)),
        name="mscan",
    )(flags, sc)


def _bcast_lane(tile, l, width):
    return jnp.broadcast_to(tile[:, l:l + 1], (tile.shape[0], width))


def _rowscale(x, b):
    w = b.shape[1]
    return jnp.concatenate([x[:, c0:c0 + w] * b for c0 in range(0, x.shape[1], w)], axis=1)


def _mlstm_chunk(q, k, v, bc_b, mt_b, e_b, r_row, m_s, a_c, tri, c_ref, n_ref, l):
    L = q.shape[0]
    c_state = c_ref[l]
    n_state = n_ref[l]
    w = jnp.exp(jnp.where(tri, (bc_b - mt_b) + r_row, -jnp.inf))
    a_t = jnp.exp((bc_b + m_s) - mt_b)
    s = lax.dot_general(q, k.astype(BF16), (((1,), (1,)), ((), ())), preferred_element_type=F32)
    scores = s * w
    n_rows = jnp.broadcast_to(n_state.astype(BF16), (L, n_state.shape[1]))
    qn = lax.dot_general(q, n_rows, (((1,), (1,)), ((), ())), preferred_element_type=F32)
    den = a_t * qn + jnp.sum(scores, axis=1, keepdims=True)
    inv = 1.0 / jnp.maximum(jnp.abs(den), jnp.exp(-mt_b))
    num = _rowscale(_dot(q, c_state.astype(BF16)), a_t) + _dot(scores.astype(BF16), v)
    h = _rowscale(num, inv)
    kw = _rowscale(k, e_b)
    c_ref[l] = a_c * c_state + lax.dot_general(
        kw.astype(BF16), v, (((0,), (0,)), ((), ())), preferred_element_type=F32)
    n_ref[l] = a_c * n_state + jnp.sum(kw, axis=0, keepdims=True)
    return h


def _mlstm_kernel(flags_ref,
                  qf_ref, kf_ref, vf_ref, bcf_ref, gcf_ref, dmf_ref, msf_ref, rrf_ref,
                  qb_ref, kb_ref, vb_ref, bcb_ref, gcb_ref, dmb_ref, msb_ref, rrb_ref,
                  hf_ref, hb_ref, c_ref, n_ref, *, rb, heads, dh):
    L = CHUNK
    nc = rb // L
    i = pl.program_id(0)
    nb = pl.num_programs(0)

    def reset(lo):
        for l in range(lo, lo + heads):
            c_ref[l] = jnp.zeros(c_ref.shape[1:], F32)
            n_ref[l] = jnp.zeros(n_ref.shape[1:], F32)

    @pl.when(flags_ref[0, i] == 1)
    def _():
        reset(0)

    @pl.when(flags_ref[1, nb - 1 - i] == 1)
    def _():
        reset(heads)

    row = lax.broadcasted_iota(jnp.int32, (L, L), 0)
    col = lax.broadcasted_iota(jnp.int32, (L, L), 1)
    dirs = (
        (col <= row, qf_ref, kf_ref, vf_ref, bcf_ref, gcf_ref, dmf_ref, msf_ref, rrf_ref, hf_ref),
        (col >= row, qb_ref, kb_ref, vb_ref, bcb_ref, gcb_ref, dmb_ref, msb_ref, rrb_ref, hb_ref),
    )

    def body(c, carry):
        for d, (tri, q_ref, k_ref, v_ref, bc_ref, gc_ref, dm_ref, ms_ref, rr_ref, h_ref) in enumerate(dirs):
            cc = c if d == 0 else nc - 1 - c
            rows = pl.ds(pl.multiple_of(cc * L, L), L)
            ms = ms_ref[cc]
            rr = rr_ref[cc]
            bct = bc_ref[rows, :]
            m_t = jnp.maximum(bct + ms[0:1, :], dm_ref[rows, :])
            e_t = jnp.exp(gc_ref[rows, :] - ms[1:2, :])
            for hd in range(heads):
                l = d * heads + hd
                cols = slice(hd * dh, (hd + 1) * dh)
                h = _mlstm_chunk(
                    q_ref[rows, cols], k_ref[rows, cols], v_ref[rows, cols],
                    _bcast_lane(bct, l, L), _bcast_lane(m_t, l, L), _bcast_lane(e_t, l, L),
                    rr[l:l + 1, :], ms[0:1, l:l + 1], ms[2:3, l:l + 1],
                    tri, c_ref, n_ref, l)
                h_ref[rows, cols] = h
        return carry

    lax.fori_loop(0, nc, body, 0)


def _mlstm(flags, q, k, v, bc, gc, dm, ms, rr, *, rb, heads):
    t, mw = q.shape
    dh = mw // heads
    nb = t // rb
    ncb = rb // CHUNK
    fwd = lambda i, f: (i, 0)
    bwd = lambda i, f: (nb - 1 - i, 0)
    fwd3 = lambda i, f: (i, 0, 0)
    bwd3 = lambda i, f: (nb - 1 - i, 0, 0)

    def specs(m2, m3):
        wide = pl.BlockSpec((rb, mw), m2)
        narrow = pl.BlockSpec((rb, LANES), m2)
        per_chunk = pl.BlockSpec((ncb, 8, LANES), m3)
        return [wide, wide, wide, narrow, narrow, narrow, per_chunk, per_chunk]

    kern = functools.partial(_mlstm_kernel, rb=rb, heads=heads, dh=dh)
    return pl.pallas_call(
        kern,
        grid_spec=pltpu.PrefetchScalarGridSpec(
            num_scalar_prefetch=1, grid=(nb,),
            in_specs=specs(fwd, fwd3) + specs(bwd, bwd3),
            out_specs=[pl.BlockSpec((rb, mw), fwd), pl.BlockSpec((rb, mw), bwd)],
            scratch_shapes=[pltpu.VMEM((2 * heads, dh, dh), F32),
                            pltpu.VMEM((2 * heads, 1, dh), F32)]),
        out_shape=[jax.ShapeDtypeStruct((t, mw), F32), jax.ShapeDtypeStruct((t, mw), F32)],
        compiler_params=pltpu.CompilerParams(
            dimension_semantics=("arbitrary",), vmem_limit_bytes=VMEM_LIMIT),
        name="mlstm",
    )(flags, q, k, v, bc, gc, dm, ms, rr, q, k, v, bc, gc, dm, ms, rr)


def _combine_kernel(hf_ref, hb_ref, o_ref, g_ref, out_ref, *, heads, dh):
    g = g_ref[...]
    for hd in range(heads):
        cols = slice(hd * dh, (hd + 1) * dh)
        hs = hf_ref[:, cols] + hb_ref[:, cols]
        inv = lax.rsqrt(jnp.mean(hs * hs, axis=-1, keepdims=True) + EPS)
        out_ref[:, cols] = (hs * inv * g[:, cols] * jax.nn.sigmoid(o_ref[:, cols])).astype(BF16)


def _combine(hf, hb, o, layer, norm_g, *, tr, heads):
    t, mw = hf.shape
    spec = pl.BlockSpec((tr, mw), lambda i: (i, 0))
    return pl.pallas_call(
        functools.partial(_combine_kernel, heads=heads, dh=mw // heads),
        grid=(t // tr,),
        in_specs=[spec, spec, spec, pl.BlockSpec((None, 1, mw), lambda i: (layer, 0, 0))],
        out_specs=spec,
        out_shape=jax.ShapeDtypeStruct((t, mw), BF16),
        compiler_params=pltpu.CompilerParams(dimension_semantics=("arbitrary",)),
        name="combine",
    )(hf, hb, o, norm_g)


def _out_ffn_kernel(flags_ref,
                    xp_ref, x_ref, xn_ref, cp_ref, c_ref, cn_ref, hp_ref, h_ref, hn_ref,
                    woa_ref, wob_ref, g_ref, wug_ref, wuv_ref, cwg_ref, cwv_ref, cbg_ref, cbv_ref,
                    wd_ref, fg_ref,
                    out_ref, xs_ref, ug_ref, uv_ref, *, tm, final_norm):
    i = pl.program_id(0)
    j = pl.program_id(1)

    @pl.when(j == 0)
    def _():
        ce = jnp.concatenate([cp_ref[...], c_ref[...], cn_ref[...]], axis=0)
        he = jnp.concatenate([hp_ref[...], h_ref[...], hn_ref[...]], axis=0)
        proj = _dot(ce, woa_ref[...]) + _dot(he, wob_ref[...])
        g = g_ref[...]
        keep_prev = jnp.where(flags_ref[0, i] == 1, 0.0, 1.0)
        keep_next = jnp.where(flags_ref[1, i] == 1, 0.0, 1.0)
        x_mid = x_ref[...] + proj[HALO:HALO + tm, :]
        out_ref[...] = x_mid
        xs_ref[HALO:HALO + tm, :] = _rms(x_mid, g).astype(BF16)
        xs_ref[0:HALO, :] = (_rms(xp_ref[...] + proj[0:HALO, :], g) * keep_prev).astype(BF16)
        xs_ref[HALO + tm:, :] = (_rms(xn_ref[...] + proj[HALO + tm:, :], g) * keep_next).astype(BF16)

    xe = xs_ref[...]
    ug_ref[...] = _dot(xe, wug_ref[...])
    uv_ref[...] = _dot(xe, wuv_ref[...])

    def conv(u_ref, cw_ref, cb_ref):
        cw = cw_ref[...]
        return (cw[0:1, :] * u_ref[pl.ds(HALO - 1, tm), :]
                + cw[1:2, :] * u_ref[pl.ds(HALO, tm), :]
                + cw[2:3, :] * u_ref[pl.ds(HALO + 1, tm), :]) + cb_ref[...]

    u_g = conv(ug_ref, cwg_ref, cbg_ref)
    u_v = conv(uv_ref, cwv_ref, cbv_ref)
    hidden = (u_g * jax.nn.sigmoid(u_g) * u_v).astype(BF16)
    out_ref[...] += _dot(hidden, wd_ref[...])

    if final_norm:
        @pl.when(j == pl.num_programs(1) - 1)
        def _():
            out_ref[...] = _rms(out_ref[...], fg_ref[...])


def _out_ffn(x, conv_out, hm, flags, layer, w_out, norm_g, w_up, ffn_cw, ffn_cb, w_down, final_g,
             *, tm, bf, final_norm):
    t, d = x.shape
    cw = conv_out.shape[1]
    mw = hm.shape[1]
    d_ff = w_down.shape[1]
    nf = d_ff // bf
    nt = t // tm
    hb = tm // HALO
    n_halo = t // HALO
    prev = lambda i, j, f: (jnp.maximum(i * hb - 1, 0), 0)
    main = lambda i, j, f: (i, 0)
    nxt = lambda i, j, f: (jnp.minimum((i + 1) * hb, n_halo - 1), 0)
    once = pl.Buffered(1)

    def triple(width):
        return [pl.BlockSpec((HALO, width), prev), pl.BlockSpec((tm, width), main),
                pl.BlockSpec((HALO, width), nxt)]

    in_specs = triple(d) + triple(cw) + triple(mw) + [
        pl.BlockSpec((None, cw, d), lambda i, j, f: (layer, 0, 0), pipeline_mode=once),
        pl.BlockSpec((None, mw, d), lambda i, j, f: (layer, cw // mw, 0), pipeline_mode=once),
        pl.BlockSpec((None, 1, d), lambda i, j, f: (layer, 0, 0)),
        pl.BlockSpec((None, d, bf), lambda i, j, f: (layer, 0, j)),
        pl.BlockSpec((None, d, bf), lambda i, j, f: (layer, 0, nf + j)),
        pl.BlockSpec((None, 3, bf), lambda i, j, f: (layer, 0, j)),
        pl.BlockSpec((None, 3, bf), lambda i, j, f: (layer, 0, nf + j)),
        pl.BlockSpec((None, 1, bf), lambda i, j, f: (layer, 0, j)),
        pl.BlockSpec((None, 1, bf), lambda i, j, f: (layer, 0, nf + j)),
        pl.BlockSpec((None, bf, d), lambda i, j, f: (layer, j, 0)),
        pl.BlockSpec((1, d), lambda i, j, f: (0, 0)),
    ]
    u_buf = pltpu.VMEM((tm + 2 * HALO, bf), F32)
    kern = functools.partial(_out_ffn_kernel, tm=tm, final_norm=final_norm)
    return pl.pallas_call(
        kern,
        grid_spec=pltpu.PrefetchScalarGridSpec(
            num_scalar_prefetch=1, grid=(nt, nf), in_specs=in_specs,
            out_specs=pl.BlockSpec((tm, d), main),
            scratch_shapes=[pltpu.VMEM((tm + 2 * HALO, d), BF16), u_buf, u_buf]),
        out_shape=jax.ShapeDtypeStruct((t, d), F32),
        compiler_params=pltpu.CompilerParams(
            dimension_semantics=("arbitrary", "arbitrary"), vmem_limit_bytes=VMEM_LIMIT),
        name="out_ffn",
    )(flags, x, x, x, conv_out, conv_out, conv_out, hm, hm, hm,
      w_out, w_out, norm_g, w_up, w_up, ffn_cw, ffn_cw, ffn_cb, ffn_cb, w_down, final_g)


def _tile_sizes(seq_lens):
    g = int(np.gcd.reduce(np.asarray(seq_lens)))
    tm = min(512, g)
    rb = min(512, g)
    return tm, rb


def _trunk(x, seq_lens, norm_mix_g, w_in, gate_b, conv_w, conv_norm_g, mlstm_norm_g, w_out,
           norm_ffn_g, w_up, ffn_conv_w, ffn_conv_b, w_down, final_norm_g, tm=None, rb=None):
    depth, d, n_in = w_in.shape
    cw = conv_w.shape[-1]
    mw = mlstm_norm_g.shape[-1]
    heads = MLSTM_HEADS
    assert cw == mw and n_in == 3 * cw + 4 * mw + 4 * heads
    d_ff = w_down.shape[1]
    auto_tm, auto_rb = _tile_sizes(seq_lens)
    tm = tm or auto_tm
    rb = rb or auto_rb
    bw = 256 if cw % 256 == 0 else LANES
    bf = 512 if d_ff % 512 == 0 else LANES

    flags_tm = _boundary_flags(seq_lens, tm)
    flags_rb = _boundary_flags(seq_lens, rb)
    flags_ch = _boundary_flags(seq_lens, CHUNK)

    w_in_b = w_in.astype(BF16)
    w_out_b = w_out.astype(BF16)
    w_up_b = w_up.astype(BF16)
    w_down_b = w_down.astype(BF16)
    g0 = 3 * cw + 4 * mw
    i_cols = np.concatenate([np.arange(0, heads), np.arange(2 * heads, 3 * heads)]) + g0
    f_cols = i_cols + heads
    pad = ((0, 0), (0, 0), (0, LANES - 2 * heads))
    w_gi = jnp.pad(w_in[:, :, i_cols], pad).astype(BF16)
    w_gf = jnp.pad(w_in[:, :, f_cols], pad).astype(BF16)
    b_gi = jnp.pad(gate_b[:, i_cols - g0], pad[1:])[:, None, :]
    b_gf = jnp.pad(gate_b[:, f_cols - g0], pad[1:])[:, None, :]

    norm_mix_g = norm_mix_g[:, None, :]
    conv_norm_g = conv_norm_g[:, None, :]
    mlstm_norm_g = mlstm_norm_g[:, None, :]
    norm_ffn_g = norm_ffn_g[:, None, :]
    ffn_conv_b = ffn_conv_b[:, None, :]
    final_g = final_norm_g[None, :]

    for layer in range(depth):
        conv_out, q, k, v, o, gi, gf = _in_proj(
            x, flags_tm, layer, norm_mix_g, w_in_b, w_gi, w_gf, b_gi, b_gf, conv_w, conv_norm_g,
            tm=tm, bw=bw)
        bc, gc, dm, sc, rr = _gate_prep(gi, gf, rb=rb, heads=heads)
        ms = _mscan(flags_ch, sc, heads=heads)
        hf, hb = _mlstm(flags_rb, q, k, v, bc, gc, dm, ms, rr, rb=rb, heads=heads)
        hm = _combine(hf, hb, o, layer, mlstm_norm_g, tr=tm, heads=heads)
        x = _out_ffn(x, conv_out, hm, flags_tm, layer, w_out_b, norm_ffn_g, w_up_b, ffn_conv_w,
                     ffn_conv_b, w_down_b, final_g, tm=tm, bf=bf, final_norm=(layer == depth - 1))
    return x


def kernel(x_prompt, x_sample, norm_mix_g, w_in, gate_b, conv_w, conv_norm_g, mlstm_norm_g, w_out,
           norm_ffn_g, w_up, ffn_conv_w, ffn_conv_b, w_down, final_norm_g):
    bp, sp, d = x_prompt.shape
    bs, ss, _ = x_sample.shape
    seq_lens = [sp] * bp + [ss] * bs
    x = jnp.concatenate([x_prompt.reshape(bp * sp, d), x_sample.reshape(bs * ss, d)], axis=0)
    y = _trunk(x, seq_lens, norm_mix_g, w_in, gate_b, conv_w, conv_norm_g, mlstm_norm_g, w_out,
               norm_ffn_g, w_up, ffn_conv_w, ffn_conv_b, w_down, final_norm_g)
    return (y[:bp * sp].reshape(bp, sp, d), y[bp * sp:].reshape(bs, ss, d))
```

```python
import functools

import numpy as np
import jax
import jax.numpy as jnp
from jax import lax
from jax.experimental import pallas as pl
from jax.experimental.pallas import tpu as pltpu

EPS = 1e-6
CHUNK = 128
CONV_GROUPS = 8
MLSTM_HEADS = 4
LANES = 128
HALO = 16
VMEM_LIMIT = 60 * 1024 * 1024

F32 = jnp.float32
BF16 = jnp.bfloat16


def _dot(a, b):
    return jnp.dot(a, b, preferred_element_type=F32)


def _rms(x, g):
    return x * lax.rsqrt(jnp.mean(x * x, axis=-1, keepdims=True) + EPS) * g


def _boundary_flags(seq_lens, tile):
    starts = np.cumsum([0] + list(seq_lens[:-1]))
    ends = np.cumsum(list(seq_lens))
    n = int(ends[-1]) // tile
    first = np.zeros((n,), np.int32)
    last = np.zeros((n,), np.int32)
    for s in starts:
        first[int(s) // tile] = 1
    for e in ends:
        last[int(e) // tile - 1] = 1
    return jnp.asarray(np.stack([first, last]))


def _part_ends(parts, tile):
    return tuple(int(e) for e in np.cumsum([p.shape[0] // tile for p in parts]))


def _part_row_specs(ends, tm, width, main_mode=None):
    hb = tm // HALO
    specs = []
    main_kw = {} if main_mode is None else {"pipeline_mode": main_mode}
    for p, end in enumerate(ends):
        off = ends[p - 1] if p else 0
        n = end - off
        prev = lambda i, j, f, off=off, n=n: (jnp.clip((i - off) * hb - 1, 0, n * hb - 1), 0)
        main = lambda i, j, f, off=off, n=n: (jnp.clip(i - off, 0, n - 1), 0)
        nxt = lambda i, j, f, off=off, n=n: (jnp.clip((i - off + 1) * hb, 0, n * hb - 1), 0)
        specs += [pl.BlockSpec((HALO, width), prev), pl.BlockSpec((tm, width), main, **main_kw),
                  pl.BlockSpec((HALO, width), nxt)]
    return specs


def _pick_part(i, ends, refs):
    val = refs[-1][...]
    for p in range(len(ends) - 2, -1, -1):
        val = jnp.where(i < ends[p], refs[p][...], val)
    return val


def _in_proj_kernel(flags_ref, *refs, tm, gw, q_scale, ends):
    n_x = 3 * len(ends)
    x_refs = refs[:n_x]
    (g_ref, wcb_ref, wcc_ref, wch_ref, wq_ref, wk_ref, wv_ref, wo_ref,
     wgi_ref, wgf_ref, bgi_ref, bgf_ref, cw_ref, cg_ref,
     conv_ref, q_ref, k_ref, v_ref, o_ref, gi_ref, gf_ref, xs_ref, z_ref) = refs[n_x:]
    i = pl.program_id(0)
    j = pl.program_id(1)

    @pl.when(j == 0)
    def _():
        g = g_ref[...]
        keep_prev = jnp.where(flags_ref[0, i] == 1, 0.0, 1.0)
        keep_next = jnp.where(flags_ref[1, i] == 1, 0.0, 1.0)
        x_prev, x_main, x_next = (_pick_part(i, ends, x_refs[k::3]) for k in range(3))
        xs_ref[0:HALO, :] = (_rms(x_prev, g) * keep_prev).astype(BF16)
        xs_ref[HALO:HALO + tm, :] = _rms(x_main, g).astype(BF16)
        xs_ref[HALO + tm:, :] = (_rms(x_next, g) * keep_next).astype(BF16)
        xm0 = xs_ref[HALO:HALO + tm, :]
        gi_ref[...] = _dot(xm0, wgi_ref[...]) + bgi_ref[...]
        gf_ref[...] = _dot(xm0, wgf_ref[...]) + bgf_ref[...]

    xm = xs_ref[HALO:HALO + tm, :]
    xe = xs_ref[...]

    z_ref[...] = _dot(xe, wcc_ref[...]) * _dot(xe, wch_ref[...])
    cw = cw_ref[...]
    conv = (cw[0:1, :] * z_ref[pl.ds(HALO - 1, tm), :]
            + cw[1:2, :] * z_ref[pl.ds(HALO, tm), :]
            + cw[2:3, :] * z_ref[pl.ds(HALO + 1, tm), :])
    y = _dot(xm, wcb_ref[...]) * conv
    cg = cg_ref[...]
    for c0 in range(0, y.shape[1], gw):
        yg = y[:, c0:c0 + gw]
        inv = lax.rsqrt(jnp.mean(yg * yg, axis=-1, keepdims=True) + EPS)
        conv_ref[:, c0:c0 + gw] = (yg * inv * cg[:, c0:c0 + gw]).astype(BF16)

    q_ref[...] = (_dot(xm, wq_ref[...]) * q_scale).astype(BF16)
    k_ref[...] = _dot(xm, wk_ref[...])
    v_ref[...] = _dot(xm, wv_ref[...]).astype(BF16)
    o_ref[...] = _dot(xm, wo_ref[...])


def _in_proj(x_parts, flags, layer, norm_g, w_in, w_gi, w_gf, b_gi, b_gf, conv_w, conv_g, *, tm, bw):
    d = x_parts[0].shape[1]
    ends = _part_ends(x_parts, tm)
    nt = ends[-1]
    t = nt * tm
    cw_total = conv_w.shape[-1]
    nj = cw_total // bw
    gw = cw_total // CONV_GROUPS
    dh = cw_total // MLSTM_HEADS

    def wspec(k):
        return pl.BlockSpec((None, d, bw), lambda i, j, f, k=k: (layer, 0, k * nj + j))

    col = lambda i, j, f: (i, j)
    const3 = lambda i, j, f: (layer, 0, 0)
    in_specs = _part_row_specs(ends, tm, d) + [
        pl.BlockSpec((None, 1, d), const3),
        wspec(0), wspec(1), wspec(2), wspec(3), wspec(4), wspec(5), wspec(6),
        pl.BlockSpec((None, d, LANES), const3),
        pl.BlockSpec((None, d, LANES), const3),
        pl.BlockSpec((None, 1, LANES), const3),
        pl.BlockSpec((None, 1, LANES), const3),
        pl.BlockSpec((None, 3, bw), lambda i, j, f: (layer, 0, j)),
        pl.BlockSpec((None, 1, bw), lambda i, j, f: (layer, 0, j)),
    ]
    out_specs = [
        pl.BlockSpec((tm, bw), col),
        pl.BlockSpec((tm, bw), col),
        pl.BlockSpec((tm, bw), col),
        pl.BlockSpec((tm, bw), col),
        pl.BlockSpec((tm, bw), col),
        pl.BlockSpec((tm, LANES), lambda i, j, f: (i, 0)),
        pl.BlockSpec((tm, LANES), lambda i, j, f: (i, 0)),
    ]
    out_shape = [
        jax.ShapeDtypeStruct((t, cw_total), BF16),
        jax.ShapeDtypeStruct((t, cw_total), BF16),
        jax.ShapeDtypeStruct((t, cw_total), F32),
        jax.ShapeDtypeStruct((t, cw_total), BF16),
        jax.ShapeDtypeStruct((t, cw_total), F32),
        jax.ShapeDtypeStruct((t, LANES), F32),
        jax.ShapeDtypeStruct((t, LANES), F32),
    ]
    x_args = [p for p in x_parts for _ in range(3)]
    kern = functools.partial(_in_proj_kernel, tm=tm, gw=gw, q_scale=float(dh) ** -0.5, ends=ends)
    return pl.pallas_call(
        kern,
        grid_spec=pltpu.PrefetchScalarGridSpec(
            num_scalar_prefetch=1, grid=(nt, nj), in_specs=in_specs, out_specs=out_specs,
            scratch_shapes=[pltpu.VMEM((tm + 2 * HALO, d), BF16),
                            pltpu.VMEM((tm + 2 * HALO, bw), F32)]),
        out_shape=out_shape,
        compiler_params=pltpu.CompilerParams(
            dimension_semantics=("arbitrary", "arbitrary"), vmem_limit_bytes=VMEM_LIMIT),
        name="in_proj",
    )(flags, *x_args, norm_g, w_in, w_in, w_in, w_in, w_in, w_in, w_in,
      w_gi, w_gf, b_gi, b_gf, conv_w, conv_g)


def _split3(x):
    hi = x.astype(BF16)
    r1 = x - hi.astype(F32)
    mid = r1.astype(BF16)
    lo = (r1 - mid.astype(F32)).astype(BF16)
    return hi, mid, lo


def _gate_prep_kernel(gi_ref, gf_ref, bc_ref, gc_ref, dm_ref, sc_ref, rr_ref, *, rb, heads):
    L = CHUNK
    row = lax.broadcasted_iota(jnp.int32, (L, L), 0)
    col = lax.broadcasted_iota(jnp.int32, (L, L), 1)
    causal = (col <= row, col >= row)
    tri_pre = causal[0].astype(BF16)
    tri_suf = causal[1].astype(BF16)
    lane = lax.broadcasted_iota(jnp.int32, (1, LANES), 1)
    fwd_lane = lane < heads
    for c in range(rb // L):
        rows = pl.ds(c * L, L)
        gi = gi_ref[rows, :]
        gf = gf_ref[rows, :]
        logf = jnp.minimum(gf, 0.0) - jnp.log1p(jnp.exp(-jnp.abs(gf)))
        hi, mid, lo = _split3(logf)
        pre = _dot(tri_pre, hi) + _dot(tri_pre, mid) + _dot(tri_pre, lo)
        suf = _dot(tri_suf, hi) + _dot(tri_suf, mid) + _dot(tri_suf, lo)
        bc = jnp.where(fwd_lane, pre, suf)
        bl = jnp.where(fwd_lane, pre[L - 1:L, :], suf[0:1, :])
        gc = bl - bc + gi
        gm = jnp.max(gc, axis=0, keepdims=True)
        r_rows = (gi - bc).T[0:8, :]
        cm = jnp.zeros((L, LANES), F32)
        for l in range(2 * heads):
            masked = jnp.where(causal[l // heads], r_rows[l:l + 1, :], -jnp.inf)
            cm = jnp.where(lane == l, jnp.max(masked, axis=1, keepdims=True), cm)
        bc_ref[rows, :] = bc
        gc_ref[rows, :] = gc
        dm_ref[rows, :] = bc + cm
        sc_ref[c] = jnp.concatenate([bl, gm, jnp.zeros((6, LANES), F32)], axis=0)
        rr_ref[c] = r_rows


def _gate_prep(gi, gf, *, rb, heads):
    t = gi.shape[0]
    nb = t // rb
    ncb = rb // CHUNK
    row_spec = pl.BlockSpec((rb, LANES), lambda i: (i, 0))
    ch_spec = pl.BlockSpec((ncb, 8, LANES), lambda i: (i, 0, 0))
    return pl.pallas_call(
        functools.partial(_gate_prep_kernel, rb=rb, heads=heads),
        grid=(nb,),
        in_specs=[row_spec, row_spec],
        out_specs=[row_spec, row_spec, row_spec, ch_spec, ch_spec],
        out_shape=[jax.ShapeDtypeStruct((t, LANES), F32),
                   jax.ShapeDtypeStruct((t, LANES), F32),
                   jax.ShapeDtypeStruct((t, LANES), F32),
                   jax.ShapeDtypeStruct((t // CHUNK, 8, LANES), F32),
                   jax.ShapeDtypeStruct((t // CHUNK, 8, LANES), F32)],
        compiler_params=pltpu.CompilerParams(dimension_semantics=("arbitrary",)),
        name="gate_prep",
    )(gi, gf)


def _mscan_kernel(flags_ref, sc_ref, ms_ref, *, nch, heads):
    fwd_lane = lax.broadcasted_iota(jnp.int32, (1, LANES), 1) < heads
    pad = jnp.zeros((5, LANES), F32)

    def step(c, m, reset):
        m = jnp.where(reset, 0.0, m)
        sc = sc_ref[c]
        bl = sc[0:1, :]
        m_new = jnp.maximum(bl + m, sc[1:2, :])
        return m, m_new, jnp.exp(bl + m - m_new)

    def fwd(c, m):
        m, m_new, a_c = step(c, m, flags_ref[0, c] == 1)
        ms_ref[c] = jnp.concatenate([m, m_new, a_c, pad], axis=0)
        return m_new

    def bwd(k, m):
        c = nch - 1 - k
        m, m_new, a_c = step(c, m, flags_ref[1, c] == 1)
        ms_ref[c] = jnp.where(fwd_lane, ms_ref[c], jnp.concatenate([m, m_new, a_c, pad], axis=0))
        return m_new

    zero = jnp.zeros((1, LANES), F32)
    lax.fori_loop(0, nch, fwd, zero)
    lax.fori_loop(0, nch, bwd, zero)


def _mscan(flags, sc, *, heads):
    nch = sc.shape[0]
    full = pl.BlockSpec(sc.shape, lambda i, f: (0, 0, 0))
    return pl.pallas_call(
        functools.partial(_mscan_kernel, nch=nch, heads=heads),
        grid_spec=pltpu.PrefetchScalarGridSpec(
            num_scalar_prefetch=1, grid=(1,), in_specs=[full], out_specs=full),
        out_shape=jax.ShapeDtypeStruct(sc.shape, F32),
        compiler_params=pltpu.CompilerParams(dimension_semantics=("arbitrary",)),
        name="mscan",
    )(flags, sc)


def _bcast_lane(tile, l, width):
    return jnp.broadcast_to(tile[:, l:l + 1], (tile.shape[0], width))


def _rowscale(x, b):
    w = b.shape[1]
    return jnp.concatenate([x[:, c0:c0 + w] * b for c0 in range(0, x.shape[1], w)], axis=1)


def _mlstm_chunk(q, k, v, bc_b, mt_b, e_b, r_row, m_s, a_c, tri, c_ref, n_ref, l):
    L = q.shape[0]
    c_state = c_ref[l]
    n_state = n_ref[l]
    w = jnp.exp(jnp.where(tri, (bc_b - mt_b) + r_row, -jnp.inf))
    a_t = jnp.exp((bc_b + m_s) - mt_b)
    s = lax.dot_general(q, k.astype(BF16), (((1,), (1,)), ((), ())), preferred_element_type=F32)
    scores = s * w
    n_rows = jnp.broadcast_to(n_state.astype(BF16), (L, n_state.shape[1]))
    qn = lax.dot_general(q, n_rows, (((1,), (1,)), ((), ())), preferred_element_type=F32)
    den = a_t * qn + jnp.sum(scores, axis=1, keepdims=True)
    inv = 1.0 / jnp.maximum(jnp.abs(den), jnp.exp(-mt_b))
    num = _rowscale(_dot(q, c_state.astype(BF16)), a_t) + _dot(scores.astype(BF16), v)
    h = _rowscale(num, inv)
    kw = _rowscale(k, e_b)
    c_ref[l] = a_c * c_state + lax.dot_general(
        kw.astype(BF16), v, (((0,), (0,)), ((), ())), preferred_element_type=F32)
    n_ref[l] = a_c * n_state + jnp.sum(kw, axis=0, keepdims=True)
    return h


def _mlstm_kernel(flags_ref,
                  qf_ref, kf_ref, vf_ref, bcf_ref, gcf_ref, dmf_ref, msf_ref, rrf_ref,
                  qb_ref, kb_ref, vb_ref, bcb_ref, gcb_ref, dmb_ref, msb_ref, rrb_ref,
                  hf_ref, hb_ref, c_ref, n_ref, *, rb, heads, dh):
    L = CHUNK
    nc = rb // L
    i = pl.program_id(0)
    nb = pl.num_programs(0)

    def reset(lo):
        for l in range(lo, lo + heads):
            c_ref[l] = jnp.zeros(c_ref.shape[1:], F32)
            n_ref[l] = jnp.zeros(n_ref.shape[1:], F32)

    @pl.when(flags_ref[0, i] == 1)
    def _():
        reset(0)

    @pl.when(flags_ref[1, nb - 1 - i] == 1)
    def _():
        reset(heads)

    row = lax.broadcasted_iota(jnp.int32, (L, L), 0)
    col = lax.broadcasted_iota(jnp.int32, (L, L), 1)
    dirs = (
        (col <= row, qf_ref, kf_ref, vf_ref, bcf_ref, gcf_ref, dmf_ref, msf_ref, rrf_ref, hf_ref),
        (col >= row, qb_ref, kb_ref, vb_ref, bcb_ref, gcb_ref, dmb_ref, msb_ref, rrb_ref, hb_ref),
    )

    def body(c, carry):
        for d, (tri, q_ref, k_ref, v_ref, bc_ref, gc_ref, dm_ref, ms_ref, rr_ref, h_ref) in enumerate(dirs):
            cc = c if d == 0 else nc - 1 - c
            rows = pl.ds(pl.multiple_of(cc * L, L), L)
            ms = ms_ref[cc]
            rr = rr_ref[cc]
            bct = bc_ref[rows, :]
            m_t = jnp.maximum(bct + ms[0:1, :], dm_ref[rows, :])
            e_t = jnp.exp(gc_ref[rows, :] - ms[1:2, :])
            for hd in range(heads):
                l = d * heads + hd
                cols = slice(hd * dh, (hd + 1) * dh)
                h = _mlstm_chunk(
                    q_ref[rows, cols], k_ref[rows, cols], v_ref[rows, cols],
                    _bcast_lane(bct, l, L), _bcast_lane(m_t, l, L), _bcast_lane(e_t, l, L),
                    rr[l:l + 1, :], ms[0:1, l:l + 1], ms[2:3, l:l + 1],
                    tri, c_ref, n_ref, l)
                h_ref[rows, cols] = h
        return carry

    lax.fori_loop(0, nc, body, 0)


def _mlstm(flags, q, k, v, bc, gc, dm, ms, rr, *, rb, heads):
    t, mw = q.shape
    dh = mw // heads
    nb = t // rb
    ncb = rb // CHUNK
    fwd = lambda i, f: (i, 0)
    bwd = lambda i, f: (nb - 1 - i, 0)
    fwd3 = lambda i, f: (i, 0, 0)
    bwd3 = lambda i, f: (nb - 1 - i, 0, 0)

    def specs(m2, m3):
        wide = pl.BlockSpec((rb, mw), m2)
        narrow = pl.BlockSpec((rb, LANES), m2)
        per_chunk = pl.BlockSpec((ncb, 8, LANES), m3)
        return [wide, wide, wide, narrow, narrow, narrow, per_chunk, per_chunk]

    kern = functools.partial(_mlstm_kernel, rb=rb, heads=heads, dh=dh)
    return pl.pallas_call(
        kern,
        grid_spec=pltpu.PrefetchScalarGridSpec(
            num_scalar_prefetch=1, grid=(nb,),
            in_specs=specs(fwd, fwd3) + specs(bwd, bwd3),
            out_specs=[pl.BlockSpec((rb, mw), fwd), pl.BlockSpec((rb, mw), bwd)],
            scratch_shapes=[pltpu.VMEM((2 * heads, dh, dh), F32),
                            pltpu.VMEM((2 * heads, 1, dh), F32)]),
        out_shape=[jax.ShapeDtypeStruct((t, mw), F32), jax.ShapeDtypeStruct((t, mw), F32)],
        compiler_params=pltpu.CompilerParams(
            dimension_semantics=("arbitrary",), vmem_limit_bytes=VMEM_LIMIT),
        name="mlstm",
    )(flags, q, k, v, bc, gc, dm, ms, rr, q, k, v, bc, gc, dm, ms, rr)


def _combine_kernel(hf_ref, hb_ref, o_ref, g_ref, out_ref, *, heads, dh):
    g = g_ref[...]
    for hd in range(heads):
        cols = slice(hd * dh, (hd + 1) * dh)
        hs = hf_ref[:, cols] + hb_ref[:, cols]
        inv = lax.rsqrt(jnp.mean(hs * hs, axis=-1, keepdims=True) + EPS)
        out_ref[:, cols] = (hs * inv * g[:, cols] * jax.nn.sigmoid(o_ref[:, cols])).astype(BF16)


def _combine(hf, hb, o, layer, norm_g, *, tr, heads):
    t, mw = hf.shape
    spec = pl.BlockSpec((tr, mw), lambda i: (i, 0))
    return pl.pallas_call(
        functools.partial(_combine_kernel, heads=heads, dh=mw // heads),
        grid=(t // tr,),
        in_specs=[spec, spec, spec, pl.BlockSpec((None, 1, mw), lambda i: (layer, 0, 0))],
        out_specs=spec,
        out_shape=jax.ShapeDtypeStruct((t, mw), BF16),
        compiler_params=pltpu.CompilerParams(dimension_semantics=("arbitrary",)),
        name="combine",
    )(hf, hb, o, norm_g)


def _out_ffn_kernel(flags_ref, *refs, tm, final_norm, in_ends, out_ends):
    n_x = 3 * len(in_ends)
    n_out = len(out_ends)
    x_refs = refs[:n_x]
    (cp_ref, c_ref, cn_ref, hp_ref, h_ref, hn_ref, woa_ref, wob_ref, g_ref,
     wug_ref, wuv_ref, cwg_ref, cwv_ref, cbg_ref, cbv_ref, wd_ref, fg_ref) = refs[n_x:n_x + 17]
    out_refs = refs[n_x + 17:n_x + 17 + n_out]
    xs_ref, ug_ref, uv_ref = refs[n_x + 17 + n_out:n_x + 20 + n_out]
    out_ref = out_refs[0] if n_out == 1 else refs[-1]
    i = pl.program_id(0)
    j = pl.program_id(1)

    @pl.when(j == 0)
    def _():
        ce = jnp.concatenate([cp_ref[...], c_ref[...], cn_ref[...]], axis=0)
        he = jnp.concatenate([hp_ref[...], h_ref[...], hn_ref[...]], axis=0)
        proj = _dot(ce, woa_ref[...]) + _dot(he, wob_ref[...])
        g = g_ref[...]
        keep_prev = jnp.where(flags_ref[0, i] == 1, 0.0, 1.0)
        keep_next = jnp.where(flags_ref[1, i] == 1, 0.0, 1.0)
        x_prev, x_main, x_next = (_pick_part(i, in_ends, x_refs[k::3]) for k in range(3))
        x_mid = x_main + proj[HALO:HALO + tm, :]
        out_ref[...] = x_mid
        xs_ref[HALO:HALO + tm, :] = _rms(x_mid, g).astype(BF16)
        xs_ref[0:HALO, :] = (_rms(x_prev + proj[0:HALO, :], g) * keep_prev).astype(BF16)
        xs_ref[HALO + tm:, :] = (_rms(x_next + proj[HALO + tm:, :], g) * keep_next).astype(BF16)

    xe = xs_ref[...]
    ug_ref[...] = _dot(xe, wug_ref[...])
    uv_ref[...] = _dot(xe, wuv_ref[...])

    def conv(u_ref, cw_ref, cb_ref):
        cw = cw_ref[...]
        return (cw[0:1, :] * u_ref[pl.ds(HALO - 1, tm), :]
                + cw[1:2, :] * u_ref[pl.ds(HALO, tm), :]
                + cw[2:3, :] * u_ref[pl.ds(HALO + 1, tm), :]) + cb_ref[...]

    u_g = conv(ug_ref, cwg_ref, cbg_ref)
    u_v = conv(uv_ref, cwv_ref, cbv_ref)
    hidden = (u_g * jax.nn.sigmoid(u_g) * u_v).astype(BF16)
    out_ref[...] += _dot(hidden, wd_ref[...])

    last_j = j == pl.num_programs(1) - 1
    if n_out == 1:
        if final_norm:
            @pl.when(last_j)
            def _():
                out_ref[...] = _rms(out_ref[...], fg_ref[...])
    else:
        for p, part_ref in enumerate(out_refs):
            lo = out_ends[p - 1] if p else 0
            in_part = jnp.logical_and(i >= lo, i < out_ends[p])

            @pl.when(jnp.logical_and(last_j, in_part))
            def _(part_ref=part_ref):
                acc = out_ref[...]
                part_ref[...] = _rms(acc, fg_ref[...]) if final_norm else acc


def _out_ffn(x_parts, conv_out, hm, flags, layer, w_out, norm_g, w_up, ffn_cw, ffn_cb, w_down,
             final_g, *, tm, bf, final_norm, out_rows):
    d = x_parts[0].shape[1]
    t, cw = conv_out.shape
    mw = hm.shape[1]
    d_ff = w_down.shape[1]
    nf = d_ff // bf
    nt = t // tm
    in_ends = _part_ends(x_parts, tm)
    out_ends = tuple(int(e) for e in np.cumsum([r // tm for r in out_rows]))
    assert in_ends[-1] == nt and out_ends[-1] == nt
    once = pl.Buffered(1)

    def triple(width):
        return _part_row_specs((nt,), tm, width)

    out_specs = []
    out_kw = {"pipeline_mode": once} if len(out_ends) > 1 else {}
    for p, end in enumerate(out_ends):
        off = out_ends[p - 1] if p else 0
        out_specs.append(pl.BlockSpec(
            (tm, d), lambda i, j, f, off=off, n=end - off: (jnp.clip(i - off, 0, n - 1), 0),
            **out_kw))
    out_shape = [jax.ShapeDtypeStruct((r, d), F32) for r in out_rows]
    acc = [] if len(out_rows) == 1 else [pltpu.VMEM((tm, d), F32)]

    x_mode = once if len(in_ends) > 1 else None
    in_specs = _part_row_specs(in_ends, tm, d, x_mode) + triple(cw) + triple(mw) + [
        pl.BlockSpec((None, cw, d), lambda i, j, f: (layer, 0, 0), pipeline_mode=once),
        pl.BlockSpec((None, mw, d), lambda i, j, f: (layer, cw // mw, 0), pipeline_mode=once),
        pl.BlockSpec((None, 1, d), lambda i, j, f: (layer, 0, 0)),
        pl.BlockSpec((None, d, bf), lambda i, j, f: (layer, 0, j)),
        pl.BlockSpec((None, d, bf), lambda i, j, f: (layer, 0, nf + j)),
        pl.BlockSpec((None, 3, bf), lambda i, j, f: (layer, 0, j)),
        pl.BlockSpec((None, 3, bf), lambda i, j, f: (layer, 0, nf + j)),
        pl.BlockSpec((None, 1, bf), lambda i, j, f: (layer, 0, j)),
        pl.BlockSpec((None, 1, bf), lambda i, j, f: (layer, 0, nf + j)),
        pl.BlockSpec((None, bf, d), lambda i, j, f: (layer, j, 0)),
        pl.BlockSpec((1, d), lambda i, j, f: (0, 0)),
    ]
    u_buf = pltpu.VMEM((tm + 2 * HALO, bf), F32)
    x_args = [p for p in x_parts for _ in range(3)]
    kern = functools.partial(_out_ffn_kernel, tm=tm, final_norm=final_norm,
                             in_ends=in_ends, out_ends=out_ends)
    return pl.pallas_call(
        kern,
        grid_spec=pltpu.PrefetchScalarGridSpec(
            num_scalar_prefetch=1, grid=(nt, nf), in_specs=in_specs, out_specs=out_specs,
            scratch_shapes=[pltpu.VMEM((tm + 2 * HALO, d), BF16), u_buf, u_buf] + acc),
        out_shape=out_shape,
        compiler_params=pltpu.CompilerParams(
            dimension_semantics=("arbitrary", "arbitrary"), vmem_limit_bytes=VMEM_LIMIT),
        name="out_ffn",
    )(flags, *x_args, conv_out, conv_out, conv_out, hm, hm, hm,
      w_out, w_out, norm_g, w_up, w_up, ffn_cw, ffn_cw, ffn_cb, ffn_cb, w_down, final_g)


def _tile_sizes(seq_lens):
    g = int(np.gcd.reduce(np.asarray(seq_lens)))
    tm_in = min(1024, g)
    tm = min(512, g)
    rb = min(512, g)
    return tm_in, tm, rb


def _trunk(x_parts, seq_lens, norm_mix_g, w_in, gate_b, conv_w, conv_norm_g, mlstm_norm_g, w_out,
           norm_ffn_g, w_up, ffn_conv_w, ffn_conv_b, w_down, final_norm_g,
           tm_in=None, tm=None, rb=None):
    depth, d, n_in = w_in.shape
    cw = conv_w.shape[-1]
    mw = mlstm_norm_g.shape[-1]
    heads = MLSTM_HEADS
    assert cw == mw and n_in == 3 * cw + 4 * mw + 4 * heads
    d_ff = w_down.shape[1]
    auto_tm_in, auto_tm, auto_rb = _tile_sizes(seq_lens)
    tm_in = tm_in or auto_tm_in
    tm = tm or auto_tm
    rb = rb or auto_rb
    bw = 256 if cw % 256 == 0 else LANES
    bf = 512 if d_ff % 512 == 0 else LANES

    flags_in = _boundary_flags(seq_lens, tm_in)
    flags_tm = _boundary_flags(seq_lens, tm)
    flags_rb = _boundary_flags(seq_lens, rb)
    flags_ch = _boundary_flags(seq_lens, CHUNK)

    w_in_b = w_in.astype(BF16)
    w_out_b = w_out.astype(BF16)
    w_up_b = w_up.astype(BF16)
    w_down_b = w_down.astype(BF16)
    g0 = 3 * cw + 4 * mw
    i_cols = np.concatenate([np.arange(0, heads), np.arange(2 * heads, 3 * heads)]) + g0
    f_cols = i_cols + heads
    pad = ((0, 0), (0, 0), (0, LANES - 2 * heads))
    w_gi = jnp.pad(w_in[:, :, i_cols], pad).astype(BF16)
    w_gf = jnp.pad(w_in[:, :, f_cols], pad).astype(BF16)
    b_gi = jnp.pad(gate_b[:, i_cols - g0], pad[1:])[:, None, :]
    b_gf = jnp.pad(gate_b[:, f_cols - g0], pad[1:])[:, None, :]

    norm_mix_g = norm_mix_g[:, None, :]
    conv_norm_g = conv_norm_g[:, None, :]
    mlstm_norm_g = mlstm_norm_g[:, None, :]
    norm_ffn_g = norm_ffn_g[:, None, :]
    ffn_conv_b = ffn_conv_b[:, None, :]
    final_g = final_norm_g[None, :]

    part_rows = [p.shape[0] for p in x_parts]
    for layer in range(depth):
        last = layer == depth - 1
        big = len(x_parts) == 1
        conv_out, q, k, v, o, gi, gf = _in_proj(
            x_parts, flags_in if big else flags_tm, layer, norm_mix_g, w_in_b, w_gi, w_gf, b_gi,
            b_gf, conv_w, conv_norm_g, tm=tm_in if big else tm, bw=bw)
        bc, gc, dm, sc, rr = _gate_prep(gi, gf, rb=rb, heads=heads)
        ms = _mscan(flags_ch, sc, heads=heads)
        hf, hb = _mlstm(flags_rb, q, k, v, bc, gc, dm, ms, rr, rb=rb, heads=heads)
        hm = _combine(hf, hb, o, layer, mlstm_norm_g, tr=tm, heads=heads)
        x_parts = _out_ffn(x_parts, conv_out, hm, flags_tm, layer, w_out_b, norm_ffn_g, w_up_b,
                           ffn_conv_w, ffn_conv_b, w_down_b, final_g, tm=tm, bf=bf, final_norm=last,
                           out_rows=part_rows if last else [sum(part_rows)])
    return x_parts


def kernel(x_prompt, x_sample, norm_mix_g, w_in, gate_b, conv_w, conv_norm_g, mlstm_norm_g, w_out,
           norm_ffn_g, w_up, ffn_conv_w, ffn_conv_b, w_down, final_norm_g):
    bp, sp, d = x_prompt.shape
    bs, ss, _ = x_sample.shape
    seq_lens = [sp] * bp + [ss] * bs
    x_parts = [x_prompt.reshape(bp * sp, d), x_sample.reshape(bs * ss, d)]
    y_prompt, y_sample = _trunk(
        x_parts, seq_lens, norm_mix_g, w_in, gate_b, conv_w, conv_norm_g, mlstm_norm_g, w_out,
        norm_ffn_g, w_up, ffn_conv_w, ffn_conv_b, w_down, final_norm_g)
    return (y_prompt.reshape(bp, sp, d), y_sample.reshape(bs, ss, d))
```

```python
import functools

import numpy as np
import jax
import jax.numpy as jnp
from jax import lax
from jax.experimental import pallas as pl
from jax.experimental.pallas import tpu as pltpu

EPS = 1e-6
CHUNK = 128
CONV_GROUPS = 8
MLSTM_HEADS = 4
LANES = 128
HALO = 16
VMEM_LIMIT = 60 * 1024 * 1024

F32 = jnp.float32
BF16 = jnp.bfloat16


def _dot(a, b):
    return jnp.dot(a, b, preferred_element_type=F32)


def _rms(x, g):
    return x * lax.rsqrt(jnp.mean(x * x, axis=-1, keepdims=True) + EPS) * g


def _boundary_flags(seq_lens, tile):
    starts = np.cumsum([0] + list(seq_lens[:-1]))
    ends = np.cumsum(list(seq_lens))
    n = int(ends[-1]) // tile
    first = np.zeros((n,), np.int32)
    last = np.zeros((n,), np.int32)
    for s in starts:
        first[int(s) // tile] = 1
    for e in ends:
        last[int(e) // tile - 1] = 1
    return jnp.asarray(np.stack([first, last]))


def _part_ends(parts, tile):
    return tuple(int(e) for e in np.cumsum([p.shape[0] // tile for p in parts]))


def _part_row_specs(ends, tm, width, main_mode=None):
    hb = tm // HALO
    specs = []
    main_kw = {} if main_mode is None else {"pipeline_mode": main_mode}
    for p, end in enumerate(ends):
        off = ends[p - 1] if p else 0
        n = end - off
        prev = lambda i, j, f, off=off, n=n: (jnp.clip((i - off) * hb - 1, 0, n * hb - 1), 0)
        main = lambda i, j, f, off=off, n=n: (jnp.clip(i - off, 0, n - 1), 0)
        nxt = lambda i, j, f, off=off, n=n: (jnp.clip((i - off + 1) * hb, 0, n * hb - 1), 0)
        specs += [pl.BlockSpec((HALO, width), prev), pl.BlockSpec((tm, width), main, **main_kw),
                  pl.BlockSpec((HALO, width), nxt)]
    return specs


def _pick_part(i, ends, refs):
    val = refs[-1][...]
    for p in range(len(ends) - 2, -1, -1):
        val = jnp.where(i < ends[p], refs[p][...], val)
    return val


def _in_proj_kernel(flags_ref, *refs, tm, gw, q_scale, ends):
    n_x = 3 * len(ends)
    x_refs = refs[:n_x]
    (g_ref, wcb_ref, wcc_ref, wch_ref, wq_ref, wk_ref, wv_ref, wo_ref,
     wgi_ref, wgf_ref, bgi_ref, bgf_ref, cw_ref, cg_ref,
     conv_ref, q_ref, k_ref, v_ref, o_ref, gi_ref, gf_ref, xs_ref, z_ref) = refs[n_x:]
    i = pl.program_id(0)
    j = pl.program_id(1)

    @pl.when(j == 0)
    def _():
        g = g_ref[...]
        keep_prev = jnp.where(flags_ref[0, i] == 1, 0.0, 1.0)
        keep_next = jnp.where(flags_ref[1, i] == 1, 0.0, 1.0)
        x_prev, x_main, x_next = (_pick_part(i, ends, x_refs[k::3]) for k in range(3))
        xs_ref[0:HALO, :] = (_rms(x_prev, g) * keep_prev).astype(BF16)
        xs_ref[HALO:HALO + tm, :] = _rms(x_main, g).astype(BF16)
        xs_ref[HALO + tm:, :] = (_rms(x_next, g) * keep_next).astype(BF16)
        xm0 = xs_ref[HALO:HALO + tm, :]
        gi_ref[...] = _dot(xm0, wgi_ref[...]) + bgi_ref[...]
        gf_ref[...] = _dot(xm0, wgf_ref[...]) + bgf_ref[...]

    xm = xs_ref[HALO:HALO + tm, :]
    xe = xs_ref[...]

    z_ref[...] = _dot(xe, wcc_ref[...]) * _dot(xe, wch_ref[...])
    cw = cw_ref[...]
    conv = (cw[0:1, :] * z_ref[pl.ds(HALO - 1, tm), :]
            + cw[1:2, :] * z_ref[pl.ds(HALO, tm), :]
            + cw[2:3, :] * z_ref[pl.ds(HALO + 1, tm), :])
    y = _dot(xm, wcb_ref[...]) * conv
    cg = cg_ref[...]
    for c0 in range(0, y.shape[1], gw):
        yg = y[:, c0:c0 + gw]
        inv = lax.rsqrt(jnp.mean(yg * yg, axis=-1, keepdims=True) + EPS)
        conv_ref[:, c0:c0 + gw] = (yg * inv * cg[:, c0:c0 + gw]).astype(BF16)

    q_ref[...] = (_dot(xm, wq_ref[...]) * q_scale).astype(BF16)
    k_ref[...] = _dot(xm, wk_ref[...])
    v_ref[...] = _dot(xm, wv_ref[...]).astype(BF16)
    o_ref[...] = _dot(xm, wo_ref[...])


def _in_proj(x_parts, flags, layer, norm_g, w_in, w_gi, w_gf, b_gi, b_gf, conv_w, conv_g, *, tm, bw):
    d = x_parts[0].shape[1]
    ends = _part_ends(x_parts, tm)
    nt = ends[-1]
    t = nt * tm
    cw_total = conv_w.shape[-1]
    nj = cw_total // bw
    gw = cw_total // CONV_GROUPS
    dh = cw_total // MLSTM_HEADS

    def wspec(k):
        return pl.BlockSpec((None, d, bw), lambda i, j, f, k=k: (layer, 0, k * nj + j))

    col = lambda i, j, f: (i, j)
    const3 = lambda i, j, f: (layer, 0, 0)
    in_specs = _part_row_specs(ends, tm, d) + [
        pl.BlockSpec((None, 1, d), const3),
        wspec(0), wspec(1), wspec(2), wspec(3), wspec(4), wspec(5), wspec(6),
        pl.BlockSpec((None, d, LANES), const3),
        pl.BlockSpec((None, d, LANES), const3),
        pl.BlockSpec((None, 1, LANES), const3),
        pl.BlockSpec((None, 1, LANES), const3),
        pl.BlockSpec((None, 3, bw), lambda i, j, f: (layer, 0, j)),
        pl.BlockSpec((None, 1, bw), lambda i, j, f: (layer, 0, j)),
    ]
    out_specs = [
        pl.BlockSpec((tm, bw), col),
        pl.BlockSpec((tm, bw), col),
        pl.BlockSpec((tm, bw), col),
        pl.BlockSpec((tm, bw), col),
        pl.BlockSpec((tm, bw), col),
        pl.BlockSpec((tm, LANES), lambda i, j, f: (i, 0)),
        pl.BlockSpec((tm, LANES), lambda i, j, f: (i, 0)),
    ]
    out_shape = [
        jax.ShapeDtypeStruct((t, cw_total), BF16),
        jax.ShapeDtypeStruct((t, cw_total), BF16),
        jax.ShapeDtypeStruct((t, cw_total), F32),
        jax.ShapeDtypeStruct((t, cw_total), BF16),
        jax.ShapeDtypeStruct((t, cw_total), F32),
        jax.ShapeDtypeStruct((t, LANES), F32),
        jax.ShapeDtypeStruct((t, LANES), F32),
    ]
    x_args = [p for p in x_parts for _ in range(3)]
    kern = functools.partial(_in_proj_kernel, tm=tm, gw=gw, q_scale=float(dh) ** -0.5, ends=ends)
    return pl.pallas_call(
        kern,
        grid_spec=pltpu.PrefetchScalarGridSpec(
            num_scalar_prefetch=1, grid=(nt, nj), in_specs=in_specs, out_specs=out_specs,
            scratch_shapes=[pltpu.VMEM((tm + 2 * HALO, d), BF16),
                            pltpu.VMEM((tm + 2 * HALO, bw), F32)]),
        out_shape=out_shape,
        compiler_params=pltpu.CompilerParams(
            dimension_semantics=("arbitrary", "arbitrary"), vmem_limit_bytes=VMEM_LIMIT),
        name="in_proj",
    )(flags, *x_args, norm_g, w_in, w_in, w_in, w_in, w_in, w_in, w_in,
      w_gi, w_gf, b_gi, b_gf, conv_w, conv_g)


def _split3(x):
    hi = x.astype(BF16)
    r1 = x - hi.astype(F32)
    mid = r1.astype(BF16)
    lo = (r1 - mid.astype(F32)).astype(BF16)
    return hi, mid, lo


def _gate_prep_kernel(gi_ref, gf_ref, bc_ref, gc_ref, dm_ref, sc_ref, rr_ref, *, rb, heads):
    L = CHUNK
    row = lax.broadcasted_iota(jnp.int32, (L, L), 0)
    col = lax.broadcasted_iota(jnp.int32, (L, L), 1)
    causal = (col <= row, col >= row)
    tri_pre = causal[0].astype(BF16)
    tri_suf = causal[1].astype(BF16)
    lane = lax.broadcasted_iota(jnp.int32, (1, LANES), 1)
    fwd_lane = lane < heads
    for c in range(rb // L):
        rows = pl.ds(c * L, L)
        gi = gi_ref[rows, :]
        gf = gf_ref[rows, :]
        logf = jnp.minimum(gf, 0.0) - jnp.log1p(jnp.exp(-jnp.abs(gf)))
        hi, mid, lo = _split3(logf)
        pre = _dot(tri_pre, hi) + _dot(tri_pre, mid) + _dot(tri_pre, lo)
        suf = _dot(tri_suf, hi) + _dot(tri_suf, mid) + _dot(tri_suf, lo)
        bc = jnp.where(fwd_lane, pre, suf)
        bl = jnp.where(fwd_lane, pre[L - 1:L, :], suf[0:1, :])
        gc = bl - bc + gi
        gm = jnp.max(gc, axis=0, keepdims=True)
        r_rows = (gi - bc).T[0:8, :]
        cm = jnp.zeros((L, LANES), F32)
        for l in range(2 * heads):
            masked = jnp.where(causal[l // heads], r_rows[l:l + 1, :], -jnp.inf)
            cm = jnp.where(lane == l, jnp.max(masked, axis=1, keepdims=True), cm)
        bc_ref[rows, :] = bc
        gc_ref[rows, :] = gc
        dm_ref[rows, :] = bc + cm
        sc_ref[c] = jnp.concatenate([bl, gm, jnp.zeros((6, LANES), F32)], axis=0)
        rr_ref[c] = r_rows


def _gate_prep(gi, gf, *, rb, heads):
    t = gi.shape[0]
    nb = t // rb
    ncb = rb // CHUNK
    row_spec = pl.BlockSpec((rb, LANES), lambda i: (i, 0))
    ch_spec = pl.BlockSpec((ncb, 8, LANES), lambda i: (i, 0, 0))
    return pl.pallas_call(
        functools.partial(_gate_prep_kernel, rb=rb, heads=heads),
        grid=(nb,),
        in_specs=[row_spec, row_spec],
        out_specs=[row_spec, row_spec, row_spec, ch_spec, ch_spec],
        out_shape=[jax.ShapeDtypeStruct((t, LANES), F32),
                   jax.ShapeDtypeStruct((t, LANES), F32),
                   jax.ShapeDtypeStruct((t, LANES), F32),
                   jax.ShapeDtypeStruct((t // CHUNK, 8, LANES), F32),
                   jax.ShapeDtypeStruct((t // CHUNK, 8, LANES), F32)],
        compiler_params=pltpu.CompilerParams(dimension_semantics=("arbitrary",)),
        name="gate_prep",
    )(gi, gf)


def _mscan_kernel(flags_ref, sc_ref, ms_ref, *, nch, heads):
    fwd_lane = lax.broadcasted_iota(jnp.int32, (1, LANES), 1) < heads
    pad = jnp.zeros((5, LANES), F32)

    def step(c, m, reset):
        m = jnp.where(reset, 0.0, m)
        sc = sc_ref[c]
        bl = sc[0:1, :]
        m_new = jnp.maximum(bl + m, sc[1:2, :])
        return m, m_new, jnp.exp(bl + m - m_new)

    def fwd(c, m):
        m, m_new, a_c = step(c, m, flags_ref[0, c] == 1)
        ms_ref[c] = jnp.concatenate([m, m_new, a_c, pad], axis=0)
        return m_new

    def bwd(k, m):
        c = nch - 1 - k
        m, m_new, a_c = step(c, m, flags_ref[1, c] == 1)
        ms_ref[c] = jnp.where(fwd_lane, ms_ref[c], jnp.concatenate([m, m_new, a_c, pad], axis=0))
        return m_new

    zero = jnp.zeros((1, LANES), F32)
    lax.fori_loop(0, nch, fwd, zero)
    lax.fori_loop(0, nch, bwd, zero)


def _mscan(flags, sc, *, heads):
    nch = sc.shape[0]
    full = pl.BlockSpec(sc.shape, lambda i, f: (0, 0, 0))
    return pl.pallas_call(
        functools.partial(_mscan_kernel, nch=nch, heads=heads),
        grid_spec=pltpu.PrefetchScalarGridSpec(
            num_scalar_prefetch=1, grid=(1,), in_specs=[full], out_specs=full),
        out_shape=jax.ShapeDtypeStruct(sc.shape, F32),
        compiler_params=pltpu.CompilerParams(dimension_semantics=("arbitrary",)),
        name="mscan",
    )(flags, sc)


def _bcast_lane(tile, l, width):
    return jnp.broadcast_to(tile[:, l:l + 1], (tile.shape[0], width))


def _rowscale(x, b):
    w = b.shape[1]
    return jnp.concatenate([x[:, c0:c0 + w] * b for c0 in range(0, x.shape[1], w)], axis=1)


def _mlstm_chunk(q, k, v, bc_b, mt_b, e_b, r_row, m_s, a_c, tri, c_ref, n_ref, l):
    L = q.shape[0]
    c_state = c_ref[l]
    n_state = n_ref[l]
    w = jnp.exp(jnp.where(tri, (bc_b - mt_b) + r_row, -jnp.inf))
    a_t = jnp.exp((bc_b + m_s) - mt_b)
    s = lax.dot_general(q, k.astype(BF16), (((1,), (1,)), ((), ())), preferred_element_type=F32)
    scores = s * w
    n_rows = jnp.broadcast_to(n_state.astype(BF16), (L, n_state.shape[1]))
    qn = lax.dot_general(q, n_rows, (((1,), (1,)), ((), ())), preferred_element_type=F32)
    den = a_t * qn + jnp.sum(scores, axis=1, keepdims=True)
    inv = 1.0 / jnp.maximum(jnp.abs(den), jnp.exp(-mt_b))
    num = _rowscale(_dot(q, c_state.astype(BF16)), a_t) + _dot(scores.astype(BF16), v)
    h = _rowscale(num, inv)
    kw = _rowscale(k, e_b)
    c_ref[l] = a_c * c_state + lax.dot_general(
        kw.astype(BF16), v, (((0,), (0,)), ((), ())), preferred_element_type=F32)
    n_ref[l] = a_c * n_state + jnp.sum(kw, axis=0, keepdims=True)
    return h


def _mlstm_kernel(flags_ref,
                  qf_ref, kf_ref, vf_ref, bcf_ref, gcf_ref, dmf_ref, msf_ref, rrf_ref,
                  qb_ref, kb_ref, vb_ref, bcb_ref, gcb_ref, dmb_ref, msb_ref, rrb_ref,
                  hf_ref, hb_ref, c_ref, n_ref, *, rb, heads, dh):
    L = CHUNK
    nc = rb // L
    i = pl.program_id(0)
    nb = pl.num_programs(0)

    def reset(lo):
        for l in range(lo, lo + heads):
            c_ref[l] = jnp.zeros(c_ref.shape[1:], F32)
            n_ref[l] = jnp.zeros(n_ref.shape[1:], F32)

    @pl.when(flags_ref[0, i] == 1)
    def _():
        reset(0)

    @pl.when(flags_ref[1, nb - 1 - i] == 1)
    def _():
        reset(heads)

    row = lax.broadcasted_iota(jnp.int32, (L, L), 0)
    col = lax.broadcasted_iota(jnp.int32, (L, L), 1)
    dirs = (
        (col <= row, qf_ref, kf_ref, vf_ref, bcf_ref, gcf_ref, dmf_ref, msf_ref, rrf_ref, hf_ref),
        (col >= row, qb_ref, kb_ref, vb_ref, bcb_ref, gcb_ref, dmb_ref, msb_ref, rrb_ref, hb_ref),
    )

    def body(c, carry):
        for d, (tri, q_ref, k_ref, v_ref, bc_ref, gc_ref, dm_ref, ms_ref, rr_ref, h_ref) in enumerate(dirs):
            cc = c if d == 0 else nc - 1 - c
            rows = pl.ds(pl.multiple_of(cc * L, L), L)
            ms = ms_ref[cc]
            rr = rr_ref[cc]
            bct = bc_ref[rows, :]
            m_t = jnp.maximum(bct + ms[0:1, :], dm_ref[rows, :])
            e_t = jnp.exp(gc_ref[rows, :] - ms[1:2, :])
            for hd in range(heads):
                l = d * heads + hd
                cols = slice(hd * dh, (hd + 1) * dh)
                h = _mlstm_chunk(
                    q_ref[rows, cols], k_ref[rows, cols], v_ref[rows, cols],
                    _bcast_lane(bct, l, L), _bcast_lane(m_t, l, L), _bcast_lane(e_t, l, L),
                    rr[l:l + 1, :], ms[0:1, l:l + 1], ms[2:3, l:l + 1],
                    tri, c_ref, n_ref, l)
                h_ref[rows, cols] = h
        return carry

    lax.fori_loop(0, nc, body, 0)


def _mlstm(flags, q, k, v, bc, gc, dm, ms, rr, *, rb, heads):
    t, mw = q.shape
    dh = mw // heads
    nb = t // rb
    ncb = rb // CHUNK
    fwd = lambda i, f: (i, 0)
    bwd = lambda i, f: (nb - 1 - i, 0)
    fwd3 = lambda i, f: (i, 0, 0)
    bwd3 = lambda i, f: (nb - 1 - i, 0, 0)

    def specs(m2, m3):
        wide = pl.BlockSpec((rb, mw), m2)
        narrow = pl.BlockSpec((rb, LANES), m2)
        per_chunk = pl.BlockSpec((ncb, 8, LANES), m3)
        return [wide, wide, wide, narrow, narrow, narrow, per_chunk, per_chunk]

    kern = functools.partial(_mlstm_kernel, rb=rb, heads=heads, dh=dh)
    return pl.pallas_call(
        kern,
        grid_spec=pltpu.PrefetchScalarGridSpec(
            num_scalar_prefetch=1, grid=(nb,),
            in_specs=specs(fwd, fwd3) + specs(bwd, bwd3),
            out_specs=[pl.BlockSpec((rb, mw), fwd), pl.BlockSpec((rb, mw), bwd)],
            scratch_shapes=[pltpu.VMEM((2 * heads, dh, dh), F32),
                            pltpu.VMEM((2 * heads, 1, dh), F32)]),
        out_shape=[jax.ShapeDtypeStruct((t, mw), F32), jax.ShapeDtypeStruct((t, mw), F32)],
        compiler_params=pltpu.CompilerParams(
            dimension_semantics=("arbitrary",), vmem_limit_bytes=VMEM_LIMIT),
        name="mlstm",
    )(flags, q, k, v, bc, gc, dm, ms, rr, q, k, v, bc, gc, dm, ms, rr)


def _combine_kernel(hf_ref, hb_ref, o_ref, g_ref, out_ref, *, heads, dh):
    g = g_ref[...]
    for hd in range(heads):
        cols = slice(hd * dh, (hd + 1) * dh)
        hs = hf_ref[:, cols] + hb_ref[:, cols]
        inv = lax.rsqrt(jnp.mean(hs * hs, axis=-1, keepdims=True) + EPS)
        out_ref[:, cols] = (hs * inv * g[:, cols] * jax.nn.sigmoid(o_ref[:, cols])).astype(BF16)


def _combine(hf, hb, o, layer, norm_g, *, tr, heads):
    t, mw = hf.shape
    spec = pl.BlockSpec((tr, mw), lambda i: (i, 0))
    return pl.pallas_call(
        functools.partial(_combine_kernel, heads=heads, dh=mw // heads),
        grid=(t // tr,),
        in_specs=[spec, spec, spec, pl.BlockSpec((None, 1, mw), lambda i: (layer, 0, 0))],
        out_specs=spec,
        out_shape=jax.ShapeDtypeStruct((t, mw), BF16),
        compiler_params=pltpu.CompilerParams(dimension_semantics=("arbitrary",)),
        name="combine",
    )(hf, hb, o, norm_g)


def _out_ffn_kernel(flags_ref, *refs, tm, final_norm, in_ends):
    n_x = 3 * len(in_ends)
    x_refs = refs[:n_x]
    (cp_ref, c_ref, cn_ref, hp_ref, h_ref, hn_ref, woa_ref, wob_ref, g_ref,
     wug_ref, wuv_ref, cwg_ref, cwv_ref, cbg_ref, cbv_ref, wd_ref, fg_ref,
     out_ref, xs_ref, ug_ref, uv_ref) = refs[n_x:]
    i = pl.program_id(0)
    j = pl.program_id(1)

    @pl.when(j == 0)
    def _():
        ce = jnp.concatenate([cp_ref[...], c_ref[...], cn_ref[...]], axis=0)
        he = jnp.concatenate([hp_ref[...], h_ref[...], hn_ref[...]], axis=0)
        proj = _dot(ce, woa_ref[...]) + _dot(he, wob_ref[...])
        g = g_ref[...]
        keep_prev = jnp.where(flags_ref[0, i] == 1, 0.0, 1.0)
        keep_next = jnp.where(flags_ref[1, i] == 1, 0.0, 1.0)
        x_prev, x_main, x_next = (_pick_part(i, in_ends, x_refs[k::3]) for k in range(3))
        x_mid = x_main + proj[HALO:HALO + tm, :]
        out_ref[...] = x_mid
        xs_ref[HALO:HALO + tm, :] = _rms(x_mid, g).astype(BF16)
        xs_ref[0:HALO, :] = (_rms(x_prev + proj[0:HALO, :], g) * keep_prev).astype(BF16)
        xs_ref[HALO + tm:, :] = (_rms(x_next + proj[HALO + tm:, :], g) * keep_next).astype(BF16)

    xe = xs_ref[...]
    ug_ref[...] = _dot(xe, wug_ref[...])
    uv_ref[...] = _dot(xe, wuv_ref[...])

    def conv(u_ref, cw_ref, cb_ref):
        cw = cw_ref[...]
        return (cw[0:1, :] * u_ref[pl.ds(HALO - 1, tm), :]
                + cw[1:2, :] * u_ref[pl.ds(HALO, tm), :]
                + cw[2:3, :] * u_ref[pl.ds(HALO + 1, tm), :]) + cb_ref[...]

    u_g = conv(ug_ref, cwg_ref, cbg_ref)
    u_v = conv(uv_ref, cwv_ref, cbv_ref)
    hidden = (u_g * jax.nn.sigmoid(u_g) * u_v).astype(BF16)
    out_ref[...] += _dot(hidden, wd_ref[...])

    if final_norm:
        @pl.when(j == pl.num_programs(1) - 1)
        def _():
            out_ref[...] = _rms(out_ref[...], fg_ref[...])


def _out_ffn(x_parts, conv_out, hm, flags, layer, w_out, norm_g, w_up, ffn_cw, ffn_cb, w_down,
             final_g, *, tm, bf, final_norm):
    d = x_parts[0].shape[1]
    t, cw = conv_out.shape
    mw = hm.shape[1]
    d_ff = w_down.shape[1]
    nf = d_ff // bf
    nt = t // tm
    in_ends = _part_ends(x_parts, tm)
    assert in_ends[-1] == nt
    once = pl.Buffered(1)

    def triple(width):
        return _part_row_specs((nt,), tm, width)

    in_specs = _part_row_specs(in_ends, tm, d) + triple(cw) + triple(mw) + [
        pl.BlockSpec((None, cw, d), lambda i, j, f: (layer, 0, 0), pipeline_mode=once),
        pl.BlockSpec((None, mw, d), lambda i, j, f: (layer, cw // mw, 0), pipeline_mode=once),
        pl.BlockSpec((None, 1, d), lambda i, j, f: (layer, 0, 0)),
        pl.BlockSpec((None, d, bf), lambda i, j, f: (layer, 0, j)),
        pl.BlockSpec((None, d, bf), lambda i, j, f: (layer, 0, nf + j)),
        pl.BlockSpec((None, 3, bf), lambda i, j, f: (layer, 0, j)),
        pl.BlockSpec((None, 3, bf), lambda i, j, f: (layer, 0, nf + j)),
        pl.BlockSpec((None, 1, bf), lambda i, j, f: (layer, 0, j)),
        pl.BlockSpec((None, 1, bf), lambda i, j, f: (layer, 0, nf + j)),
        pl.BlockSpec((None, bf, d), lambda i, j, f: (layer, j, 0)),
        pl.BlockSpec((1, d), lambda i, j, f: (0, 0)),
    ]
    u_buf = pltpu.VMEM((tm + 2 * HALO, bf), F32)
    x_args = [p for p in x_parts for _ in range(3)]
    kern = functools.partial(_out_ffn_kernel, tm=tm, final_norm=final_norm, in_ends=in_ends)
    return pl.pallas_call(
        kern,
        grid_spec=pltpu.PrefetchScalarGridSpec(
            num_scalar_prefetch=1, grid=(nt, nf), in_specs=in_specs,
            out_specs=pl.BlockSpec((tm, d), lambda i, j, f: (i, 0)),
            scratch_shapes=[pltpu.VMEM((tm + 2 * HALO, d), BF16), u_buf, u_buf]),
        out_shape=jax.ShapeDtypeStruct((t, d), F32),
        compiler_params=pltpu.CompilerParams(
            dimension_semantics=("arbitrary", "arbitrary"), vmem_limit_bytes=VMEM_LIMIT),
        name="out_ffn",
    )(flags, *x_args, conv_out, conv_out, conv_out, hm, hm, hm,
      w_out, w_out, norm_g, w_up, w_up, ffn_cw, ffn_cw, ffn_cb, ffn_cb, w_down, final_g)


def _tile_sizes(seq_lens):
    g = int(np.gcd.reduce(np.asarray(seq_lens)))
    tm_in = min(1024, g)
    tm = min(512, g)
    rb = min(512, g)
    return tm_in, tm, rb


def _trunk(x_parts, seq_lens, norm_mix_g, w_in, gate_b, conv_w, conv_norm_g, mlstm_norm_g, w_out,
           norm_ffn_g, w_up, ffn_conv_w, ffn_conv_b, w_down, final_norm_g,
           tm_in=None, tm=None, rb=None):
    depth, d, n_in = w_in.shape
    cw = conv_w.shape[-1]
    mw = mlstm_norm_g.shape[-1]
    heads = MLSTM_HEADS
    assert cw == mw and n_in == 3 * cw + 4 * mw + 4 * heads
    d_ff = w_down.shape[1]
    auto_tm_in, auto_tm, auto_rb = _tile_sizes(seq_lens)
    tm_in = tm_in or auto_tm_in
    tm = tm or auto_tm
    rb = rb or auto_rb
    bw = 256 if cw % 256 == 0 else LANES
    bf = 512 if d_ff % 512 == 0 else LANES

    flags_in = _boundary_flags(seq_lens, tm_in)
    flags_tm = _boundary_flags(seq_lens, tm)
    flags_rb = _boundary_flags(seq_lens, rb)
    flags_ch = _boundary_flags(seq_lens, CHUNK)

    w_in_b = w_in.astype(BF16)
    w_out_b = w_out.astype(BF16)
    w_up_b = w_up.astype(BF16)
    w_down_b = w_down.astype(BF16)
    g0 = 3 * cw + 4 * mw
    i_cols = np.concatenate([np.arange(0, heads), np.arange(2 * heads, 3 * heads)])
    f_cols = i_cols + heads
    pad = ((0, 0), (0, 0), (0, LANES - 2 * heads))
    w_gates = w_in[:, :, g0:]
    w_gi = jnp.pad(w_gates[:, :, i_cols], pad).astype(BF16)
    w_gf = jnp.pad(w_gates[:, :, f_cols], pad).astype(BF16)
    b_gi = jnp.pad(gate_b[:, i_cols], pad[1:])[:, None, :]
    b_gf = jnp.pad(gate_b[:, f_cols], pad[1:])[:, None, :]

    norm_mix_g = norm_mix_g[:, None, :]
    conv_norm_g = conv_norm_g[:, None, :]
    mlstm_norm_g = mlstm_norm_g[:, None, :]
    norm_ffn_g = norm_ffn_g[:, None, :]
    ffn_conv_b = ffn_conv_b[:, None, :]
    final_g = final_norm_g[None, :]

    part_rows = [p.shape[0] for p in x_parts]
    for layer in range(depth):
        last = layer == depth - 1
        big = len(x_parts) == 1
        conv_out, q, k, v, o, gi, gf = _in_proj(
            x_parts, flags_in if big else flags_tm, layer, norm_mix_g, w_in_b, w_gi, w_gf, b_gi,
            b_gf, conv_w, conv_norm_g, tm=tm_in if big else tm, bw=bw)
        bc, gc, dm, sc, rr = _gate_prep(gi, gf, rb=rb, heads=heads)
        ms = _mscan(flags_ch, sc, heads=heads)
        hf, hb = _mlstm(flags_rb, q, k, v, bc, gc, dm, ms, rr, rb=rb, heads=heads)
        hm = _combine(hf, hb, o, layer, mlstm_norm_g, tr=tm, heads=heads)
        x_parts = [_out_ffn(x_parts, conv_out, hm, flags_tm, layer, w_out_b, norm_ffn_g, w_up_b,
                            ffn_conv_w, ffn_conv_b, w_down_b, final_g, tm=tm, bf=bf,
                            final_norm=last)]
    y = x_parts[0]
    bounds = np.cumsum([0] + part_rows)
    return [y[lo:hi] for lo, hi in zip(bounds[:-1], bounds[1:])]


def kernel(x_prompt, x_sample, norm_mix_g, w_in, gate_b, conv_w, conv_norm_g, mlstm_norm_g, w_out,
           norm_ffn_g, w_up, ffn_conv_w, ffn_conv_b, w_down, final_norm_g):
    bp, sp, d = x_prompt.shape
    bs, ss, _ = x_sample.shape
    seq_lens = [sp] * bp + [ss] * bs
    x_parts = [x_prompt.reshape(bp * sp, d), x_sample.reshape(bs * ss, d)]
    y_prompt, y_sample = _trunk(
        x_parts, seq_lens, norm_mix_g, w_in, gate_b, conv_w, conv_norm_g, mlstm_norm_g, w_out,
        norm_ffn_g, w_up, ffn_conv_w, ffn_conv_b, w_down, final_norm_g)
    return (y_prompt.reshape(bp, sp, d), y_sample.reshape(bs, ss, d))
```

```python
import functools

import numpy as np
import jax
import jax.numpy as jnp
from jax import lax
from jax.experimental import pallas as pl
from jax.experimental.pallas import tpu as pltpu

EPS = 1e-6
CHUNK = 128
CONV_GROUPS = 8
MLSTM_HEADS = 4
LANES = 128
HALO = 16
VMEM_LIMIT = 60 * 1024 * 1024

F32 = jnp.float32
BF16 = jnp.bfloat16


def _dot(a, b):
    return jnp.dot(a, b, preferred_element_type=F32)


def _rms(x, g):
    return x * lax.rsqrt(jnp.mean(x * x, axis=-1, keepdims=True) + EPS) * g


def _boundary_flags(seq_lens, tile):
    starts = np.cumsum([0] + list(seq_lens[:-1]))
    ends = np.cumsum(list(seq_lens))
    n = int(ends[-1]) // tile
    first = np.zeros((n,), np.int32)
    last = np.zeros((n,), np.int32)
    for s in starts:
        first[int(s) // tile] = 1
    for e in ends:
        last[int(e) // tile - 1] = 1
    return jnp.asarray(np.stack([first, last]))


def _part_ends(parts, tile):
    return tuple(int(e) for e in np.cumsum([p.shape[0] // tile for p in parts]))


def _part_row_specs(ends, tm, width, main_mode=None):
    hb = tm // HALO
    specs = []
    main_kw = {} if main_mode is None else {"pipeline_mode": main_mode}
    for p, end in enumerate(ends):
        off = ends[p - 1] if p else 0
        n = end - off
        prev = lambda i, j, f, off=off, n=n: (jnp.clip((i - off) * hb - 1, 0, n * hb - 1), 0)
        main = lambda i, j, f, off=off, n=n: (jnp.clip(i - off, 0, n - 1), 0)
        nxt = lambda i, j, f, off=off, n=n: (jnp.clip((i - off + 1) * hb, 0, n * hb - 1), 0)
        specs += [pl.BlockSpec((HALO, width), prev), pl.BlockSpec((tm, width), main, **main_kw),
                  pl.BlockSpec((HALO, width), nxt)]
    return specs


def _pick_part(i, ends, refs):
    val = refs[-1][...]
    for p in range(len(ends) - 2, -1, -1):
        val = jnp.where(i < ends[p], refs[p][...], val)
    return val


def _in_proj_kernel(flags_ref, *refs, tm, gw, q_scale, ends):
    n_x = 3 * len(ends)
    x_refs = refs[:n_x]
    (g_ref, wcb_ref, wcc_ref, wch_ref, wq_ref, wk_ref, wv_ref, wo_ref,
     wgi_ref, wgf_ref, bgi_ref, bgf_ref, cw_ref, cg_ref,
     conv_ref, q_ref, k_ref, v_ref, o_ref, gi_ref, gf_ref, xs_ref, z_ref) = refs[n_x:]
    i = pl.program_id(0)
    j = pl.program_id(1)

    @pl.when(j == 0)
    def _():
        g = g_ref[...]
        keep_prev = jnp.where(flags_ref[0, i] == 1, 0.0, 1.0)
        keep_next = jnp.where(flags_ref[1, i] == 1, 0.0, 1.0)
        x_prev, x_main, x_next = (_pick_part(i, ends, x_refs[k::3]) for k in range(3))
        xs_ref[0:HALO, :] = (_rms(x_prev, g) * keep_prev).astype(BF16)
        xs_ref[HALO:HALO + tm, :] = _rms(x_main, g).astype(BF16)
        xs_ref[HALO + tm:, :] = (_rms(x_next, g) * keep_next).astype(BF16)
        xm0 = xs_ref[HALO:HALO + tm, :]
        gi_ref[...] = _dot(xm0, wgi_ref[...]) + bgi_ref[...]
        gf_ref[...] = _dot(xm0, wgf_ref[...]) + bgf_ref[...]

    xm = xs_ref[HALO:HALO + tm, :]
    xe = xs_ref[...]

    z_ref[...] = _dot(xe, wcc_ref[...]) * _dot(xe, wch_ref[...])
    cw = cw_ref[...]
    conv = (cw[0:1, :] * z_ref[pl.ds(HALO - 1, tm), :]
            + cw[1:2, :] * z_ref[pl.ds(HALO, tm), :]
            + cw[2:3, :] * z_ref[pl.ds(HALO + 1, tm), :])
    y = _dot(xm, wcb_ref[...]) * conv
    cg = cg_ref[...]
    for c0 in range(0, y.shape[1], gw):
        yg = y[:, c0:c0 + gw]
        inv = lax.rsqrt(jnp.mean(yg * yg, axis=-1, keepdims=True) + EPS)
        conv_ref[:, c0:c0 + gw] = (yg * inv * cg[:, c0:c0 + gw]).astype(BF16)

    q_ref[...] = (_dot(xm, wq_ref[...]) * q_scale).astype(BF16)
    k_ref[...] = _dot(xm, wk_ref[...])
    v_ref[...] = _dot(xm, wv_ref[...]).astype(BF16)
    o_ref[...] = _dot(xm, wo_ref[...])


def _in_proj(x_parts, flags, layer, norm_g, w_in, w_gi, w_gf, b_gi, b_gf, conv_w, conv_g, *, tm, bw):
    d = x_parts[0].shape[1]
    ends = _part_ends(x_parts, tm)
    nt = ends[-1]
    t = nt * tm
    cw_total = conv_w.shape[-1]
    nj = cw_total // bw
    gw = cw_total // CONV_GROUPS
    dh = cw_total // MLSTM_HEADS

    def wspec(k):
        return pl.BlockSpec((None, d, bw), lambda i, j, f, k=k: (layer, 0, k * nj + j))

    col = lambda i, j, f: (i, j)
    const3 = lambda i, j, f: (layer, 0, 0)
    in_specs = _part_row_specs(ends, tm, d) + [
        pl.BlockSpec((None, 1, d), const3),
        wspec(0), wspec(1), wspec(2), wspec(3), wspec(4), wspec(5), wspec(6),
        pl.BlockSpec((None, d, LANES), const3),
        pl.BlockSpec((None, d, LANES), const3),
        pl.BlockSpec((None, 1, LANES), const3),
        pl.BlockSpec((None, 1, LANES), const3),
        pl.BlockSpec((None, 3, bw), lambda i, j, f: (layer, 0, j)),
        pl.BlockSpec((None, 1, bw), lambda i, j, f: (layer, 0, j)),
    ]
    out_specs = [
        pl.BlockSpec((tm, bw), col),
        pl.BlockSpec((tm, bw), col),
        pl.BlockSpec((tm, bw), col),
        pl.BlockSpec((tm, bw), col),
        pl.BlockSpec((tm, bw), col),
        pl.BlockSpec((tm, LANES), lambda i, j, f: (i, 0)),
        pl.BlockSpec((tm, LANES), lambda i, j, f: (i, 0)),
    ]
    out_shape = [
        jax.ShapeDtypeStruct((t, cw_total), BF16),
        jax.ShapeDtypeStruct((t, cw_total), BF16),
        jax.ShapeDtypeStruct((t, cw_total), F32),
        jax.ShapeDtypeStruct((t, cw_total), BF16),
        jax.ShapeDtypeStruct((t, cw_total), F32),
        jax.ShapeDtypeStruct((t, LANES), F32),
        jax.ShapeDtypeStruct((t, LANES), F32),
    ]
    x_args = [p for p in x_parts for _ in range(3)]
    kern = functools.partial(_in_proj_kernel, tm=tm, gw=gw, q_scale=float(dh) ** -0.5, ends=ends)
    return pl.pallas_call(
        kern,
        grid_spec=pltpu.PrefetchScalarGridSpec(
            num_scalar_prefetch=1, grid=(nt, nj), in_specs=in_specs, out_specs=out_specs,
            scratch_shapes=[pltpu.VMEM((tm + 2 * HALO, d), BF16),
                            pltpu.VMEM((tm + 2 * HALO, bw), F32)]),
        out_shape=out_shape,
        compiler_params=pltpu.CompilerParams(
            dimension_semantics=("arbitrary", "arbitrary"), vmem_limit_bytes=VMEM_LIMIT),
        name="in_proj",
    )(flags, *x_args, norm_g, w_in, w_in, w_in, w_in, w_in, w_in, w_in,
      w_gi, w_gf, b_gi, b_gf, conv_w, conv_g)


def _split3(x):
    hi = x.astype(BF16)
    r1 = x - hi.astype(F32)
    mid = r1.astype(BF16)
    lo = (r1 - mid.astype(F32)).astype(BF16)
    return hi, mid, lo


def _gate_prep_kernel(gi_ref, gf_ref, bc_ref, gc_ref, dm_ref, sc_ref, rr_ref, *, rb, heads):
    L = CHUNK
    row = lax.broadcasted_iota(jnp.int32, (L, L), 0)
    col = lax.broadcasted_iota(jnp.int32, (L, L), 1)
    causal = (col <= row, col >= row)
    tri_pre = causal[0].astype(BF16)
    tri_suf = causal[1].astype(BF16)
    fwd_lane = lax.broadcasted_iota(jnp.int32, (1, LANES), 1) < heads
    tok = lax.broadcasted_iota(jnp.int32, (8, L), 1)
    chain = lax.broadcasted_iota(jnp.int32, (8, L), 0)
    for c in range(rb // L):
        rows = pl.ds(c * L, L)
        gi = gi_ref[rows, :]
        gf = gf_ref[rows, :]
        logf = jnp.minimum(gf, 0.0) - jnp.log1p(jnp.exp(-jnp.abs(gf)))
        hi, mid, lo = _split3(logf)
        pre = _dot(tri_pre, hi) + _dot(tri_pre, mid) + _dot(tri_pre, lo)
        suf = _dot(tri_suf, hi) + _dot(tri_suf, mid) + _dot(tri_suf, lo)
        bc = jnp.where(fwd_lane, pre, suf)
        bl = jnp.where(fwd_lane, pre[L - 1:L, :], suf[0:1, :])
        gc = bl - bc + gi
        gm = jnp.max(gc, axis=0, keepdims=True)
        r_rows = (gi - bc).T[0:8, :]
        pre_m = r_rows
        suf_m = r_rows
        step = 1
        while step < L:
            pre_m = jnp.where(tok >= step, jnp.maximum(pre_m, pltpu.roll(pre_m, step, 1)), pre_m)
            suf_m = jnp.where(tok < L - step, jnp.maximum(suf_m, pltpu.roll(suf_m, L - step, 1)), suf_m)
            step *= 2
        cm_rows = jnp.where(chain < heads, pre_m, suf_m)
        cm = jnp.concatenate([cm_rows, jnp.zeros((L - 8, L), F32)], axis=0).T
        bc_ref[rows, :] = bc
        gc_ref[rows, :] = gc
        dm_ref[rows, :] = bc + cm
        sc_ref[c] = jnp.concatenate([bl, gm, jnp.zeros((6, LANES), F32)], axis=0)
        rr_ref[c] = r_rows


def _gate_prep(gi, gf, *, rb, heads):
    t = gi.shape[0]
    nb = t // rb
    ncb = rb // CHUNK
    row_spec = pl.BlockSpec((rb, LANES), lambda i: (i, 0))
    ch_spec = pl.BlockSpec((ncb, 8, LANES), lambda i: (i, 0, 0))
    return pl.pallas_call(
        functools.partial(_gate_prep_kernel, rb=rb, heads=heads),
        grid=(nb,),
        in_specs=[row_spec, row_spec],
        out_specs=[row_spec, row_spec, row_spec, ch_spec, ch_spec],
        out_shape=[jax.ShapeDtypeStruct((t, LANES), F32),
                   jax.ShapeDtypeStruct((t, LANES), F32),
                   jax.ShapeDtypeStruct((t, LANES), F32),
                   jax.ShapeDtypeStruct((t // CHUNK, 8, LANES), F32),
                   jax.ShapeDtypeStruct((t // CHUNK, 8, LANES), F32)],
        compiler_params=pltpu.CompilerParams(dimension_semantics=("arbitrary",)),
        name="gate_prep",
    )(gi, gf)


def _mscan_kernel(flags_ref, sc_ref, ms_ref, *, nch, heads):
    fwd_lane = lax.broadcasted_iota(jnp.int32, (1, LANES), 1) < heads
    pad = jnp.zeros((5, LANES), F32)

    def step(c, m, reset):
        m = jnp.where(reset, 0.0, m)
        sc = sc_ref[c]
        bl = sc[0:1, :]
        m_new = jnp.maximum(bl + m, sc[1:2, :])
        return m, m_new, jnp.exp(bl + m - m_new)

    def fwd(c, m):
        m, m_new, a_c = step(c, m, flags_ref[0, c] == 1)
        ms_ref[c] = jnp.concatenate([m, m_new, a_c, pad], axis=0)
        return m_new

    def bwd(k, m):
        c = nch - 1 - k
        m, m_new, a_c = step(c, m, flags_ref[1, c] == 1)
        ms_ref[c] = jnp.where(fwd_lane, ms_ref[c], jnp.concatenate([m, m_new, a_c, pad], axis=0))
        return m_new

    zero = jnp.zeros((1, LANES), F32)
    lax.fori_loop(0, nch, fwd, zero)
    lax.fori_loop(0, nch, bwd, zero)


def _mscan(flags, sc, *, heads):
    nch = sc.shape[0]
    full = pl.BlockSpec(sc.shape, lambda i, f: (0, 0, 0))
    return pl.pallas_call(
        functools.partial(_mscan_kernel, nch=nch, heads=heads),
        grid_spec=pltpu.PrefetchScalarGridSpec(
            num_scalar_prefetch=1, grid=(1,), in_specs=[full], out_specs=full),
        out_shape=jax.ShapeDtypeStruct(sc.shape, F32),
        compiler_params=pltpu.CompilerParams(dimension_semantics=("arbitrary",)),
        name="mscan",
    )(flags, sc)


def _bcast_lane(tile, l, width):
    return jnp.broadcast_to(tile[:, l:l + 1], (tile.shape[0], width))


def _rowscale(x, b):
    w = b.shape[1]
    return jnp.concatenate([x[:, c0:c0 + w] * b for c0 in range(0, x.shape[1], w)], axis=1)


def _mlstm_chunk(q, k, v, bc_b, mt_b, e_b, r_row, m_s, a_c, tri, c_ref, n_ref, l):
    L = q.shape[0]
    c_state = c_ref[l]
    n_state = n_ref[l]
    w = jnp.exp(jnp.where(tri, (bc_b - mt_b) + r_row, -jnp.inf))
    a_t = jnp.exp((bc_b + m_s) - mt_b)
    s = lax.dot_general(q, k.astype(BF16), (((1,), (1,)), ((), ())), preferred_element_type=F32)
    scores = s * w
    n_rows = jnp.broadcast_to(n_state.astype(BF16), (L, n_state.shape[1]))
    qn = lax.dot_general(q, n_rows, (((1,), (1,)), ((), ())), preferred_element_type=F32)
    den = a_t * qn + jnp.sum(scores, axis=1, keepdims=True)
    inv = 1.0 / jnp.maximum(jnp.abs(den), jnp.exp(-mt_b))
    num = _rowscale(_dot(q, c_state.astype(BF16)), a_t) + _dot(scores.astype(BF16), v)
    h = _rowscale(num, inv)
    kw = _rowscale(k, e_b)
    c_ref[l] = a_c * c_state + lax.dot_general(
        kw.astype(BF16), v, (((0,), (0,)), ((), ())), preferred_element_type=F32)
    n_ref[l] = a_c * n_state + jnp.sum(kw, axis=0, keepdims=True)
    return h


def _mlstm_kernel(flags_ref,
                  qf_ref, kf_ref, vf_ref, bcf_ref, gcf_ref, dmf_ref, msf_ref, rrf_ref,
                  qb_ref, kb_ref, vb_ref, bcb_ref, gcb_ref, dmb_ref, msb_ref, rrb_ref,
                  hf_ref, hb_ref, c_ref, n_ref, *, rb, heads, dh):
    L = CHUNK
    nc = rb // L
    i = pl.program_id(0)
    nb = pl.num_programs(0)

    def reset(lo):
        for l in range(lo, lo + heads):
            c_ref[l] = jnp.zeros(c_ref.shape[1:], F32)
            n_ref[l] = jnp.zeros(n_ref.shape[1:], F32)

    @pl.when(flags_ref[0, i] == 1)
    def _():
        reset(0)

    @pl.when(flags_ref[1, nb - 1 - i] == 1)
    def _():
        reset(heads)

    row = lax.broadcasted_iota(jnp.int32, (L, L), 0)
    col = lax.broadcasted_iota(jnp.int32, (L, L), 1)
    dirs = (
        (col <= row, qf_ref, kf_ref, vf_ref, bcf_ref, gcf_ref, dmf_ref, msf_ref, rrf_ref, hf_ref),
        (col >= row, qb_ref, kb_ref, vb_ref, bcb_ref, gcb_ref, dmb_ref, msb_ref, rrb_ref, hb_ref),
    )

    def body(c, carry, d):
        for (tri, q_ref, k_ref, v_ref, bc_ref, gc_ref, dm_ref, ms_ref, rr_ref, h_ref) in dirs[d:d + 1]:
            cc = c if d == 0 else nc - 1 - c
            rows = pl.ds(pl.multiple_of(cc * L, L), L)
            ms = ms_ref[cc]
            rr = rr_ref[cc]
            bct = bc_ref[rows, :]
            m_t = jnp.maximum(bct + ms[0:1, :], dm_ref[rows, :])
            e_t = jnp.exp(gc_ref[rows, :] - ms[1:2, :])
            for hd in range(heads):
                l = d * heads + hd
                cols = slice(hd * dh, (hd + 1) * dh)
                h = _mlstm_chunk(
                    q_ref[rows, cols], k_ref[rows, cols], v_ref[rows, cols],
                    _bcast_lane(bct, l, L), _bcast_lane(m_t, l, L), _bcast_lane(e_t, l, L),
                    rr[l:l + 1, :], ms[0:1, l:l + 1], ms[2:3, l:l + 1],
                    tri, c_ref, n_ref, l)
                h_ref[rows, cols] = h
        return carry

    for d in range(2):
        lax.fori_loop(0, nc, functools.partial(body, d=d), 0)


def _mlstm(flags, q, k, v, bc, gc, dm, ms, rr, *, rb, heads):
    t, mw = q.shape
    dh = mw // heads
    nb = t // rb
    ncb = rb // CHUNK
    fwd = lambda i, f: (i, 0)
    bwd = lambda i, f: (nb - 1 - i, 0)
    fwd3 = lambda i, f: (i, 0, 0)
    bwd3 = lambda i, f: (nb - 1 - i, 0, 0)

    def specs(m2, m3):
        wide = pl.BlockSpec((rb, mw), m2)
        narrow = pl.BlockSpec((rb, LANES), m2)
        per_chunk = pl.BlockSpec((ncb, 8, LANES), m3)
        return [wide, wide, wide, narrow, narrow, narrow, per_chunk, per_chunk]

    kern = functools.partial(_mlstm_kernel, rb=rb, heads=heads, dh=dh)
    return pl.pallas_call(
        kern,
        grid_spec=pltpu.PrefetchScalarGridSpec(
            num_scalar_prefetch=1, grid=(nb,),
            in_specs=specs(fwd, fwd3) + specs(bwd, bwd3),
            out_specs=[pl.BlockSpec((rb, mw), fwd), pl.BlockSpec((rb, mw), bwd)],
            scratch_shapes=[pltpu.VMEM((2 * heads, dh, dh), F32),
                            pltpu.VMEM((2 * heads, 1, dh), F32)]),
        out_shape=[jax.ShapeDtypeStruct((t, mw), F32), jax.ShapeDtypeStruct((t, mw), F32)],
        compiler_params=pltpu.CompilerParams(
            dimension_semantics=("arbitrary",), vmem_limit_bytes=VMEM_LIMIT),
        name="mlstm",
    )(flags, q, k, v, bc, gc, dm, ms, rr, q, k, v, bc, gc, dm, ms, rr)


def _combine_kernel(hf_ref, hb_ref, o_ref, g_ref, out_ref, *, heads, dh):
    g = g_ref[...]
    for hd in range(heads):
        cols = slice(hd * dh, (hd + 1) * dh)
        hs = hf_ref[:, cols] + hb_ref[:, cols]
        inv = lax.rsqrt(jnp.mean(hs * hs, axis=-1, keepdims=True) + EPS)
        out_ref[:, cols] = (hs * inv * g[:, cols] * jax.nn.sigmoid(o_ref[:, cols])).astype(BF16)


def _combine(hf, hb, o, layer, norm_g, *, tr, heads):
    t, mw = hf.shape
    spec = pl.BlockSpec((tr, mw), lambda i: (i, 0))
    return pl.pallas_call(
        functools.partial(_combine_kernel, heads=heads, dh=mw // heads),
        grid=(t // tr,),
        in_specs=[spec, spec, spec, pl.BlockSpec((None, 1, mw), lambda i: (layer, 0, 0))],
        out_specs=spec,
        out_shape=jax.ShapeDtypeStruct((t, mw), BF16),
        compiler_params=pltpu.CompilerParams(dimension_semantics=("arbitrary",)),
        name="combine",
    )(hf, hb, o, norm_g)


def _out_ffn_kernel(flags_ref, *refs, tm, final_norm, in_ends):
    n_x = 3 * len(in_ends)
    x_refs = refs[:n_x]
    (cp_ref, c_ref, cn_ref, hp_ref, h_ref, hn_ref, woa_ref, wob_ref, g_ref,
     wug_ref, wuv_ref, cwg_ref, cwv_ref, cbg_ref, cbv_ref, wd_ref, fg_ref,
     out_ref, xs_ref, ug_ref, uv_ref) = refs[n_x:]
    i = pl.program_id(0)
    j = pl.program_id(1)

    @pl.when(j == 0)
    def _():
        ce = jnp.concatenate([cp_ref[...], c_ref[...], cn_ref[...]], axis=0)
        he = jnp.concatenate([hp_ref[...], h_ref[...], hn_ref[...]], axis=0)
        proj = _dot(ce, woa_ref[...]) + _dot(he, wob_ref[...])
        g = g_ref[...]
        keep_prev = jnp.where(flags_ref[0, i] == 1, 0.0, 1.0)
        keep_next = jnp.where(flags_ref[1, i] == 1, 0.0, 1.0)
        x_prev, x_main, x_next = (_pick_part(i, in_ends, x_refs[k::3]) for k in range(3))
        x_mid = x_main + proj[HALO:HALO + tm, :]
        out_ref[...] = x_mid
        xs_ref[HALO:HALO + tm, :] = _rms(x_mid, g).astype(BF16)
        xs_ref[0:HALO, :] = (_rms(x_prev + proj[0:HALO, :], g) * keep_prev).astype(BF16)
        xs_ref[HALO + tm:, :] = (_rms(x_next + proj[HALO + tm:, :], g) * keep_next).astype(BF16)

    xe = xs_ref[...]
    ug_ref[...] = _dot(xe, wug_ref[...])
    uv_ref[...] = _dot(xe, wuv_ref[...])

    def conv(u_ref, cw_ref, cb_ref):
        cw = cw_ref[...]
        return (cw[0:1, :] * u_ref[pl.ds(HALO - 1, tm), :]
                + cw[1:2, :] * u_ref[pl.ds(HALO, tm), :]
                + cw[2:3, :] * u_ref[pl.ds(HALO + 1, tm), :]) + cb_ref[...]

    u_g = conv(ug_ref, cwg_ref, cbg_ref)
    u_v = conv(uv_ref, cwv_ref, cbv_ref)
    hidden = (u_g * jax.nn.sigmoid(u_g) * u_v).astype(BF16)
    out_ref[...] += _dot(hidden, wd_ref[...])

    if final_norm:
        @pl.when(j == pl.num_programs(1) - 1)
        def _():
            out_ref[...] = _rms(out_ref[...], fg_ref[...])


def _out_ffn(x_parts, conv_out, hm, flags, layer, w_out, norm_g, w_up, ffn_cw, ffn_cb, w_down,
             final_g, *, tm, bf, final_norm):
    d = x_parts[0].shape[1]
    t, cw = conv_out.shape
    mw = hm.shape[1]
    d_ff = w_down.shape[1]
    nf = d_ff // bf
    nt = t // tm
    in_ends = _part_ends(x_parts, tm)
    assert in_ends[-1] == nt
    once = pl.Buffered(1)

    def triple(width):
        return _part_row_specs((nt,), tm, width)

    in_specs = _part_row_specs(in_ends, tm, d) + triple(cw) + triple(mw) + [
        pl.BlockSpec((None, cw, d), lambda i, j, f: (layer, 0, 0), pipeline_mode=once),
        pl.BlockSpec((None, mw, d), lambda i, j, f: (layer, cw // mw, 0), pipeline_mode=once),
        pl.BlockSpec((None, 1, d), lambda i, j, f: (layer, 0, 0)),
        pl.BlockSpec((None, d, bf), lambda i, j, f: (layer, 0, j)),
        pl.BlockSpec((None, d, bf), lambda i, j, f: (layer, 0, nf + j)),
        pl.BlockSpec((None, 3, bf), lambda i, j, f: (layer, 0, j)),
        pl.BlockSpec((None, 3, bf), lambda i, j, f: (layer, 0, nf + j)),
        pl.BlockSpec((None, 1, bf), lambda i, j, f: (layer, 0, j)),
        pl.BlockSpec((None, 1, bf), lambda i, j, f: (layer, 0, nf + j)),
        pl.BlockSpec((None, bf, d), lambda i, j, f: (layer, j, 0)),
        pl.BlockSpec((1, d), lambda i, j, f: (0, 0)),
    ]
    u_buf = pltpu.VMEM((tm + 2 * HALO, bf), F32)
    x_args = [p for p in x_parts for _ in range(3)]
    kern = functools.partial(_out_ffn_kernel, tm=tm, final_norm=final_norm, in_ends=in_ends)
    return pl.pallas_call(
        kern,
        grid_spec=pltpu.PrefetchScalarGridSpec(
            num_scalar_prefetch=1, grid=(nt, nf), in_specs=in_specs,
            out_specs=pl.BlockSpec((tm, d), lambda i, j, f: (i, 0)),
            scratch_shapes=[pltpu.VMEM((tm + 2 * HALO, d), BF16), u_buf, u_buf]),
        out_shape=jax.ShapeDtypeStruct((t, d), F32),
        compiler_params=pltpu.CompilerParams(
            dimension_semantics=("arbitrary", "arbitrary"), vmem_limit_bytes=VMEM_LIMIT),
        name="out_ffn",
    )(flags, *x_args, conv_out, conv_out, conv_out, hm, hm, hm,
      w_out, w_out, norm_g, w_up, w_up, ffn_cw, ffn_cw, ffn_cb, ffn_cb, w_down, final_g)


def _tile_sizes(seq_lens):
    g = int(np.gcd.reduce(np.asarray(seq_lens)))
    tm_in = min(1024, g)
    tm = min(512, g)
    rb = min(512, g)
    return tm_in, tm, rb


def _trunk(x_parts, seq_lens, norm_mix_g, w_in, gate_b, conv_w, conv_norm_g, mlstm_norm_g, w_out,
           norm_ffn_g, w_up, ffn_conv_w, ffn_conv_b, w_down, final_norm_g,
           tm_in=None, tm=None, rb=None):
    depth, d, n_in = w_in.shape
    cw = conv_w.shape[-1]
    mw = mlstm_norm_g.shape[-1]
    heads = MLSTM_HEADS
    assert cw == mw and n_in == 3 * cw + 4 * mw + 4 * heads
    d_ff = w_down.shape[1]
    auto_tm_in, auto_tm, auto_rb = _tile_sizes(seq_lens)
    tm_in = tm_in or auto_tm_in
    tm = tm or auto_tm
    rb = rb or auto_rb
    bw = 256 if cw % 256 == 0 else LANES
    bf = 512 if d_ff % 512 == 0 else LANES

    flags_in = _boundary_flags(seq_lens, tm_in)
    flags_tm = _boundary_flags(seq_lens, tm)
    flags_rb = _boundary_flags(seq_lens, rb)
    flags_ch = _boundary_flags(seq_lens, CHUNK)

    w_in_b = w_in.astype(BF16)
    w_out_b = w_out.astype(BF16)
    w_up_b = w_up.astype(BF16)
    w_down_b = w_down.astype(BF16)
    g0 = 3 * cw + 4 * mw
    i_cols = np.concatenate([np.arange(0, heads), np.arange(2 * heads, 3 * heads)])
    f_cols = i_cols + heads
    pad = ((0, 0), (0, 0), (0, LANES - 2 * heads))
    w_gates = w_in[:, :, g0:]
    w_gi = jnp.pad(w_gates[:, :, i_cols], pad).astype(BF16)
    w_gf = jnp.pad(w_gates[:, :, f_cols], pad).astype(BF16)
    b_gi = jnp.pad(gate_b[:, i_cols], pad[1:])[:, None, :]
    b_gf = jnp.pad(gate_b[:, f_cols], pad[1:])[:, None, :]

    norm_mix_g = norm_mix_g[:, None, :]
    conv_norm_g = conv_norm_g[:, None, :]
    mlstm_norm_g = mlstm_norm_g[:, None, :]
    norm_ffn_g = norm_ffn_g[:, None, :]
    ffn_conv_b = ffn_conv_b[:, None, :]
    final_g = final_norm_g[None, :]

    part_rows = [p.shape[0] for p in x_parts]
    for layer in range(depth):
        last = layer == depth - 1
        big = len(x_parts) == 1
        conv_out, q, k, v, o, gi, gf = _in_proj(
            x_parts, flags_in if big else flags_tm, layer, norm_mix_g, w_in_b, w_gi, w_gf, b_gi,
            b_gf, conv_w, conv_norm_g, tm=tm_in if big else tm, bw=bw)
        bc, gc, dm, sc, rr = _gate_prep(gi, gf, rb=rb, heads=heads)
        ms = _mscan(flags_ch, sc, heads=heads)
        hf, hb = _mlstm(flags_rb, q, k, v, bc, gc, dm, ms, rr, rb=rb, heads=heads)
        hm = _combine(hf, hb, o, layer, mlstm_norm_g, tr=tm, heads=heads)
        x_parts = [_out_ffn(x_parts, conv_out, hm, flags_tm, layer, w_out_b, norm_ffn_g, w_up_b,
                            ffn_conv_w, ffn_conv_b, w_down_b, final_g, tm=tm, bf=bf,
                            final_norm=last)]
    y = x_parts[0]
    bounds = np.cumsum([0] + part_rows)
    return [y[lo:hi] for lo, hi in zip(bounds[:-1], bounds[1:])]


def kernel(x_prompt, x_sample, norm_mix_g, w_in, gate_b, conv_w, conv_norm_g, mlstm_norm_g, w_out,
           norm_ffn_g, w_up, ffn_conv_w, ffn_conv_b, w_down, final_norm_g):
    bp, sp, d = x_prompt.shape
    bs, ss, _ = x_sample.shape
    seq_lens = [sp] * bp + [ss] * bs
    x_parts = [x_prompt.reshape(bp * sp, d), x_sample.reshape(bs * ss, d)]
    y_prompt, y_sample = _trunk(
        x_parts, seq_lens, norm_mix_g, w_in, gate_b, conv_w, conv_norm_g, mlstm_norm_g, w_out,
        norm_ffn_g, w_up, ffn_conv_w, ffn_conv_b, w_down, final_norm_g)
    return (y_prompt.reshape(bp, sp, d), y_sample.reshape(bs, ss, d))
```
